```python
import jax
import jax.numpy as jnp
from jax import lax
import numpy as np

D_MODEL = 1024
BATCH = 8
SEQ = 4096
DEPTH = 4

CTX_LEN = 256
GRID_W = 64
EPS = 1e-6

MLA_HEADS = 8
MLA_Q_RANK = 256
MLA_KV_RANK = 128
MLA_NOPE = 64
MLA_ROPE = 32
MLA_V = 64
MLA_SCALE = (MLA_NOPE + MLA_ROPE) ** -0.5
ROPE_AXIS = MLA_ROPE // 2
ROPE_BASE = 10000.0
Q_BLOCK = 128

FNET_GROUPS = 4
FNET_GROUP_W = 128

GLA_HEADS = 4
GLA_DK = 64
GLA_DV = 128
GLA_GATE_RANK = 16
GLA_TAU = 16.0
GLA_CHUNK = 64
GLA_SCALE = GLA_DK ** -0.5

N_EXPERTS = 32
TOP_K = 4
D_EXPERT = D_MODEL
SWIGLU_ALPHA = 1.702
SWIGLU_LIMIT = 7.0
MOE_BLOCK = 256

MLA_W = MLA_HEADS * MLA_V
FNET_W = FNET_GROUPS * FNET_GROUP_W
GLA_W = GLA_HEADS * GLA_DV
N_BRANCH = 3
IN_SPLITS = (MLA_Q_RANK, MLA_KV_RANK + MLA_ROPE, FNET_W, GLA_HEADS * GLA_DK, GLA_HEADS * GLA_DK, GLA_W, GLA_W, GLA_GATE_RANK, GLA_GATE_RANK, N_BRANCH * D_MODEL)
IN_WIDTH = sum(IN_SPLITS)

kernel_name = 'hybrid_mla_fourier_gla_moe_dit'


def rms_norm(x, g):
    xf = x.astype(jnp.float32)
    y = xf * lax.rsqrt(jnp.mean(xf * xf, axis=-1, keepdims=True) + EPS)
    return (y * g.astype(jnp.float32)).astype(x.dtype)


def modulate(h, shift, scale):
    return h * (1 + scale) + shift


def split_in(u):
    idx = [int(i) for i in np.cumsum(IN_SPLITS)[:-1]]
    return jnp.split(u, idx, axis=-1)


def axial_rope_angles(rows, dtype):
    row = jnp.repeat(jnp.arange(rows, dtype=jnp.float32), GRID_W)
    col = jnp.tile(jnp.arange(GRID_W, dtype=jnp.float32), rows)
    inv_freq = ROPE_BASE ** (-jnp.arange(0, ROPE_AXIS, 2, dtype=jnp.float32) / ROPE_AXIS)
    ang_r = (row[:, None] * inv_freq)[:, None, :]
    ang_c = (col[:, None] * inv_freq)[:, None, :]
    return (jnp.cos(ang_r).astype(dtype), jnp.sin(ang_r).astype(dtype),
            jnp.cos(ang_c).astype(dtype), jnp.sin(ang_c).astype(dtype))


def rope_axis(x, cos, sin):
    x1, x2 = jnp.split(x, 2, axis=-1)
    return jnp.concatenate([x1 * cos - x2 * sin, x2 * cos + x1 * sin], axis=-1)


def rope_2d(x, rope):
    cos_r, sin_r, cos_c, sin_c = rope
    xr, xc = jnp.split(x, 2, axis=-1)
    return jnp.concatenate([rope_axis(xr, cos_r, sin_r), rope_axis(xc, cos_c, sin_c)], axis=-1)


def mla_qkv(u_q, u_kv, q_norm_g, w_uq, kv_norm_g, w_ukv, rope):
    b, t, _ = u_q.shape
    q = (rms_norm(u_q, q_norm_g) @ w_uq).reshape(b, t, MLA_HEADS, MLA_NOPE + MLA_ROPE)
    c_kv, k_rope = jnp.split(u_kv, [MLA_KV_RANK], axis=-1)
    kv = (rms_norm(c_kv, kv_norm_g) @ w_ukv).reshape(b, t, MLA_HEADS, MLA_NOPE + MLA_V)
    k_nope, v = jnp.split(kv, [MLA_NOPE], axis=-1)
    q_nope, q_rope = jnp.split(q, [MLA_NOPE], axis=-1)
    k_rope = k_rope[:, :, None, :]
    if rope is not None:
        q_rope = rope_2d(q_rope, rope)
        k_rope = rope_2d(k_rope, rope)
    q = jnp.concatenate([q_nope, q_rope], axis=-1)
    k = jnp.concatenate([k_nope, jnp.broadcast_to(k_rope, (b, t, MLA_HEADS, MLA_ROPE))], axis=-1)
    return q, k, v


def attend(q, k, v):
    s = jnp.einsum('bqhd,bkhd->bhqk', q, k).astype(jnp.float32) * MLA_SCALE
    p = jax.nn.softmax(s, axis=-1).astype(v.dtype)
    return jnp.einsum('bhqk,bkhd->bqhd', p, v)


def attend_latent_blocks(q, k, v):
    b, t, h, dk = q.shape
    nb = t // Q_BLOCK
    qb = jnp.moveaxis(q.reshape(b, nb, Q_BLOCK, h, dk), 1, 0)
    o = lax.map(lambda qq: attend(qq, k, v), qb)
    return jnp.moveaxis(o, 0, 1).reshape(b, t, h * v.shape[-1])


def fourier_mix(u):
    b, t, _ = u.shape
    uf = u.astype(jnp.float32).reshape(b, t, FNET_GROUPS, FNET_GROUP_W)
    y = jnp.fft.fft2(uf, axes=(1, 3), norm='ortho').real
    return y.reshape(b, t, FNET_W).astype(u.dtype)


def gla_prepare(u_q, u_k, u_v, u_gf, u_gb, w_gf, b_gf, w_gb, b_gb):
    b, t, _ = u_q.shape
    q = u_q.reshape(b, t, GLA_HEADS, GLA_DK) * GLA_SCALE
    k = u_k.reshape(b, t, GLA_HEADS, GLA_DK)
    v = u_v.reshape(b, t, GLA_HEADS, GLA_DV)
    lf = (jax.nn.log_sigmoid((u_gf @ w_gf + b_gf).astype(jnp.float32)) / GLA_TAU).reshape(b, t, GLA_HEADS, GLA_DK)
    lb = (jax.nn.log_sigmoid((u_gb @ w_gb + b_gb).astype(jnp.float32)) / GLA_TAU).reshape(b, t, GLA_HEADS, GLA_DK)
    return q, k, v, lf, lb


def gla_scan(q, k, v, logf, s0):
    b, t, h, _ = q.shape
    nc = t // GLA_CHUNK

    def chunks(a):
        return a.astype(jnp.float32).reshape(b, nc, GLA_CHUNK, h, a.shape[-1]).transpose(1, 0, 3, 2, 4)

    lower = jnp.tril(jnp.ones((GLA_CHUNK, GLA_CHUNK), dtype=bool))[:, :, None]

    def step(state, xs):
        qc, kc, vc, fc = xs
        g = jnp.cumsum(fc, axis=2)
        o_inter = jnp.einsum('bhld,bhde->bhle', qc * jnp.exp(g), state)
        diff = jnp.where(lower, g[:, :, :, None, :] - g[:, :, None, :, :], -jnp.inf)
        a = jnp.einsum('bhtd,bhsd,bhtsd->bhts', qc, kc, jnp.exp(diff))
        o_intra = jnp.einsum('bhts,bhse->bhte', a, vc)
        g_last = g[:, :, -1:, :]
        new_state = (jnp.exp(g_last[:, :, 0, :])[..., None] * state
                     + jnp.einsum('bhsd,bhse->bhde', kc * jnp.exp(g_last - g), vc))
        return new_state, o_inter + o_intra

    s_fin, o = lax.scan(step, s0, (chunks(q), chunks(k), chunks(v), chunks(logf)))
    o = o.transpose(1, 0, 3, 2, 4).reshape(b, t, h, v.shape[-1])
    return o, s_fin


def gla_output(o, u_og, norm_g):
    b, t = u_og.shape[:2]
    o = rms_norm(o.astype(u_og.dtype), norm_g.reshape(GLA_HEADS, GLA_DV))
    return (o * jax.nn.silu(u_og.reshape(b, t, GLA_HEADS, GLA_DV))).reshape(b, t, GLA_W)


def merge_branches(o_mla, o_fn, o_gla, gate_pre, w_br_mla, w_br_fnet, w_br_gla, w_o):
    ga, gf, gg = jnp.split(jax.nn.sigmoid(gate_pre), N_BRANCH, axis=-1)
    y = ga * (o_mla @ w_br_mla) + gf * (o_fn @ w_br_fnet) + gg * (o_gla @ w_br_gla)
    return y @ w_o


def flip(a):
    return a[:, ::-1]


def token_mixers(h_ctx, h_lat, w_in, mla_q_norm_g, mla_w_uq, mla_kv_norm_g, mla_w_ukv,
                 gla_w_gate_f, gla_b_gate_f, gla_w_gate_b, gla_b_gate_b, gla_norm_g,
                 w_br_mla, w_br_fnet, w_br_gla, w_o, rope, need_ctx):
    b, n_ctx, _ = h_ctx.shape
    lq, lkv, lfn, lgq, lgk, lgv, log_, lgf, lgb, lgate = split_in(h_lat @ w_in)
    cq, ckv, cfn, cgq, cgk, cgv, cog, cgf, cgb, cgate = split_in(h_ctx @ w_in)

    q_l, k_l, v_l = mla_qkv(lq, lkv, mla_q_norm_g, mla_w_uq, mla_kv_norm_g, mla_w_ukv, rope)
    q_c, k_c, v_c = mla_qkv(cq, ckv, mla_q_norm_g, mla_w_uq, mla_kv_norm_g, mla_w_ukv, None)
    o_mla_l = attend_latent_blocks(q_l, jnp.concatenate([k_l, k_c], axis=1), jnp.concatenate([v_l, v_c], axis=1))

    o_fn_l = fourier_mix(lfn)

    gq_l, gk_l, gv_l, lf_l, lb_l = gla_prepare(lgq, lgk, lgv, lgf, lgb, gla_w_gate_f, gla_b_gate_f, gla_w_gate_b, gla_b_gate_b)
    gq_c, gk_c, gv_c, lf_c, lb_c = gla_prepare(cgq, cgk, cgv, cgf, cgb, gla_w_gate_f, gla_b_gate_f, gla_w_gate_b, gla_b_gate_b)
    zero = jnp.zeros((b, GLA_HEADS, GLA_DK, GLA_DV), jnp.float32)
    o_cf, s_f = gla_scan(gq_c, gk_c, gv_c, lf_c, zero)
    o_cb, s_b = gla_scan(flip(gq_c), flip(gk_c), flip(gv_c), flip(lb_c), zero)
    o_lf, _ = gla_scan(gq_l, gk_l, gv_l, lf_l, s_f)
    o_lb, _ = gla_scan(flip(gq_l), flip(gk_l), flip(gv_l), flip(lb_l), s_b)
    o_gla_l = gla_output(o_lf + flip(o_lb), log_, gla_norm_g)

    y_lat = merge_branches(o_mla_l, o_fn_l, o_gla_l, lgate, w_br_mla, w_br_fnet, w_br_gla, w_o)
    y_ctx = None
    if need_ctx:
        o_mla_c = attend(q_c, k_c, v_c).reshape(b, n_ctx, MLA_W)
        o_fn_c = fourier_mix(cfn)
        o_gla_c = gla_output(o_cf + flip(o_cb), cog, gla_norm_g)
        y_ctx = merge_branches(o_mla_c, o_fn_c, o_gla_c, cgate, w_br_mla, w_br_fnet, w_br_gla, w_o)
    return y_ctx, y_lat


def moe_ffn(h, router_w, router_b, w_up, b_up, w_down, b_down):
    n, d = h.shape
    logits = (h @ router_w).astype(jnp.float32) + router_b.astype(jnp.float32)
    top_val, top_idx = lax.top_k(logits, TOP_K)
    gate = jax.nn.softmax(top_val, axis=-1)
    n_assign = n * TOP_K
    flat_e = top_idx.reshape(n_assign)
    order = jnp.argsort(flat_e)
    sorted_e = flat_e[order]
    counts = jnp.bincount(flat_e, length=N_EXPERTS)
    padded = ((counts + MOE_BLOCK - 1) // MOE_BLOCK) * MOE_BLOCK
    pad_end = jnp.cumsum(padded)
    pad_start = pad_end - padded
    grp_start = jnp.cumsum(counts) - counts
    dest = pad_start[sorted_e] + (jnp.arange(n_assign) - grp_start[sorted_e])
    n_blocks = -(-n_assign // MOE_BLOCK) + N_EXPERTS
    n_rows = n_blocks * MOE_BLOCK
    row_tok = jnp.full((n_rows,), n, jnp.int32).at[dest].set((order // TOP_K).astype(jnp.int32))
    row_gate = jnp.zeros((n_rows,), jnp.float32).at[dest].set(gate.reshape(n_assign)[order])
    blk_e = jnp.minimum(jnp.searchsorted(pad_end, jnp.arange(n_blocks) * MOE_BLOCK, side='right'), N_EXPERTS - 1)
    h_pad = jnp.concatenate([h, jnp.zeros((1, d), h.dtype)], axis=0)

    def step(acc, xs):
        tok, gw, e = xs
        xb = h_pad[tok]
        up = xb @ w_up[e] + b_up[e]
        glu, lin = jnp.split(up, 2, axis=-1)
        glu = jnp.minimum(glu, SWIGLU_LIMIT)
        lin = jnp.clip(lin, -SWIGLU_LIMIT, SWIGLU_LIMIT)
        act = glu * jax.nn.sigmoid(SWIGLU_ALPHA * glu) * (lin + 1)
        yb = act @ w_down[e] + b_down[e]
        return acc.at[tok].add(yb * gw[:, None].astype(yb.dtype)), None

    acc0 = jnp.zeros((n + 1, d), h.dtype)
    acc, _ = lax.scan(step, acc0, (row_tok.reshape(n_blocks, MOE_BLOCK), row_gate.reshape(n_blocks, MOE_BLOCK), blk_e))
    return acc[:n]


def setup_inputs(seed: int = 0) -> dict:
    key = jax.random.key(seed)
    ks = jax.random.split(key, 29)
    L, D = DEPTH, D_MODEL

    def nrm(i, shape, scale):
        return jax.random.normal(ks[i], shape, jnp.float32) * scale

    return {
        'x': nrm(0, (BATCH, SEQ, D), 1.0),
        'c': nrm(1, (BATCH, D), 1.0),
        'ctx': nrm(2, (BATCH, CTX_LEN, D), 1.0),
        'c_ctx': nrm(3, (D,), 1.0),
        'w_mod': nrm(4, (L, D, 6 * D), 0.5 * D ** -0.5),
        'b_mod': nrm(5, (L, 6 * D), 0.02),
        'norm1_g': 1.0 + nrm(6, (L, D), 0.02),
        'w_in': nrm(7, (L, D, IN_WIDTH), D ** -0.5),
        'mla_q_norm_g': 1.0 + nrm(8, (L, MLA_Q_RANK), 0.02),
        'mla_w_uq': nrm(9, (L, MLA_Q_RANK, MLA_HEADS * (MLA_NOPE + MLA_ROPE)), MLA_Q_RANK ** -0.5),
        'mla_kv_norm_g': 1.0 + nrm(10, (L, MLA_KV_RANK), 0.02),
        'mla_w_ukv': nrm(11, (L, MLA_KV_RANK, MLA_HEADS * (MLA_NOPE + MLA_V)), MLA_KV_RANK ** -0.5),
        'gla_w_gate_f': nrm(12, (L, GLA_GATE_RANK, GLA_HEADS * GLA_DK), GLA_GATE_RANK ** -0.5),
        'gla_b_gate_f': nrm(13, (L, GLA_HEADS * GLA_DK), 0.1),
        'gla_w_gate_b': nrm(14, (L, GLA_GATE_RANK, GLA_HEADS * GLA_DK), GLA_GATE_RANK ** -0.5),
        'gla_b_gate_b': nrm(15, (L, GLA_HEADS * GLA_DK), 0.1),
        'gla_norm_g': 1.0 + nrm(16, (L, GLA_W), 0.02),
        'w_br_mla': nrm(17, (L, MLA_W, D), MLA_W ** -0.5),
        'w_br_fnet': nrm(18, (L, FNET_W, D), FNET_W ** -0.5),
        'w_br_gla': nrm(19, (L, GLA_W, D), GLA_W ** -0.5),
        'w_o': nrm(20, (L, D, D), D ** -0.5),
        'norm2_g': 1.0 + nrm(21, (L, D), 0.02),
        'router_w': nrm(22, (L, D, N_EXPERTS), D ** -0.5),
        'router_b': nrm(23, (L, N_EXPERTS), 0.01),
        'exp_w_up': nrm(24, (L, N_EXPERTS, D, 2 * D_EXPERT), D ** -0.5),
        'exp_b_up': nrm(25, (L, N_EXPERTS, 2 * D_EXPERT), 0.02),
        'exp_w_down': nrm(26, (L, N_EXPERTS, D_EXPERT, D), D_EXPERT ** -0.5),
        'exp_b_down': nrm(27, (L, N_EXPERTS, D), 0.02),
        'final_norm_g': 1.0 + nrm(28, (D,), 0.02),
    }


def reference(x, c, ctx, c_ctx, w_mod, b_mod, norm1_g, w_in, mla_q_norm_g, mla_w_uq, mla_kv_norm_g, mla_w_ukv,
              gla_w_gate_f, gla_b_gate_f, gla_w_gate_b, gla_b_gate_b, gla_norm_g, w_br_mla, w_br_fnet, w_br_gla,
              w_o, norm2_g, router_w, router_b, exp_w_up, exp_b_up, exp_w_down, exp_b_down, final_norm_g):
    b, n_lat, d = x.shape
    rows = n_lat // GRID_W
    rope = axial_rope_angles(rows, x.dtype)
    xc = ctx
    silu_c = jax.nn.silu(c)
    silu_cc = jax.nn.silu(c_ctx)
    for l in range(DEPTH):
        need_ctx = l < DEPTH - 1
        mod = (silu_c @ w_mod[l] + b_mod[l])[:, None, :]
        mod_c = (silu_cc @ w_mod[l] + b_mod[l])[None, None, :]
        sh1, sc1, g1, sh2, sc2, g2 = jnp.split(mod, 6, axis=-1)
        csh1, csc1, cg1, csh2, csc2, cg2 = jnp.split(mod_c, 6, axis=-1)

        h_lat = modulate(rms_norm(x, norm1_g[l]), sh1, sc1)
        h_ctx = modulate(rms_norm(xc, norm1_g[l]), csh1, csc1)
        y_ctx, y_lat = token_mixers(h_ctx, h_lat, w_in[l], mla_q_norm_g[l], mla_w_uq[l], mla_kv_norm_g[l], mla_w_ukv[l],
                                    gla_w_gate_f[l], gla_b_gate_f[l], gla_w_gate_b[l], gla_b_gate_b[l], gla_norm_g[l],
                                    w_br_mla[l], w_br_fnet[l], w_br_gla[l], w_o[l], rope, need_ctx)
        x = x + g1 * y_lat
        h_lat = modulate(rms_norm(x, norm2_g[l]), sh2, sc2)
        moe_params = (router_w[l], router_b[l], exp_w_up[l], exp_b_up[l], exp_w_down[l], exp_b_down[l])
        if need_ctx:
            xc = xc + cg1 * y_ctx
            h_ctx = modulate(rms_norm(xc, norm2_g[l]), csh2, csc2)
            n_ctx_tok = b * xc.shape[1]
            y = moe_ffn(jnp.concatenate([h_ctx.reshape(-1, d), h_lat.reshape(-1, d)], axis=0), *moe_params)
            xc = xc + cg2 * y[:n_ctx_tok].reshape(xc.shape)
            x = x + g2 * y[n_ctx_tok:].reshape(x.shape)
        else:
            x = x + g2 * moe_ffn(h_lat.reshape(-1, d), *moe_params).reshape(x.shape)
    return rms_norm(x, final_norm_g)
```

```python
import functools
import math

import jax
import jax.numpy as jnp
import numpy as np
from jax import lax
from jax.experimental import pallas as pl
from jax.experimental.pallas import tpu as pltpu

F32 = jnp.float32
BF16 = jnp.bfloat16
I32 = jnp.int32

D_MODEL = 1024
GRID_W = 64
EPS = 1e-6

MLA_HEADS = 8
MLA_Q_RANK = 256
MLA_KV_RANK = 128
MLA_NOPE = 64
MLA_ROPE = 32
MLA_V = 64
MLA_SCALE = (MLA_NOPE + MLA_ROPE) ** -0.5
ROPE_AXIS = MLA_ROPE // 2
ROPE_BASE = 10000.0
HEAD_PAD = 128

FNET_GROUPS = 4
FNET_GROUP_W = 128
FNET_W = FNET_GROUPS * FNET_GROUP_W

GLA_HEADS = 4
GLA_DK = 64
GLA_DV = 128
GLA_GATE_RANK = 16
GLA_TAU = 16.0
GLA_CHUNK = 64
GLA_SCALE = GLA_DK ** -0.5
GLA_QK_W = GLA_HEADS * GLA_DK
GLA_W = GLA_HEADS * GLA_DV

N_EXPERTS = 32
TOP_K = 4
D_EXPERT = D_MODEL
SWIGLU_ALPHA = 1.702
SWIGLU_LIMIT = 7.0
MOE_BLOCK = 256
ROUTER_PAD = 128

MLA_W = MLA_HEADS * MLA_V

VMEM_LIMIT_V7X = 56 * 1024 * 1024

C_Q = 0
C_KV = C_Q + MLA_Q_RANK
C_FN = C_KV + 256
C_GQ = C_FN + FNET_W
C_GK = C_GQ + GLA_QK_W
C_GV = C_GK + GLA_QK_W
C_OG = C_GV + GLA_W
C_GG = C_OG + GLA_W
C_GATE = C_GG + 128
IN_WIDTH_PAD = C_GATE + 3 * D_MODEL


def _cparams(sem, vmem=VMEM_LIMIT_V7X):
    return pltpu.CompilerParams(dimension_semantics=sem, vmem_limit_bytes=vmem)


def _rms(x, g):
    return x * lax.rsqrt(jnp.mean(x * x, axis=-1, keepdims=True) + EPS) * g


def _dot(a, b):
    return jnp.dot(a, b, preferred_element_type=F32)


def _dot_nt(a, b):
    return lax.dot_general(a, b, (((1,), (1,)), ((), ())), preferred_element_type=F32)


def _dot_tn(a, b):
    return lax.dot_general(a, b, (((0,), (0,)), ((), ())), preferred_element_type=F32)


def _mod_kernel(cv_ref, w_ref, b_ref, o_ref):
    cv = cv_ref[...]
    s = cv * jax.nn.sigmoid(cv)
    o_ref[...] = jnp.dot(s, w_ref[...], preferred_element_type=F32,
                         precision=lax.Precision.HIGHEST) + b_ref[...]


def _mod_all(cv, w_mod, b_mod):
    n_layers, d, w6 = w_mod.shape
    rows = cv.shape[0]
    tn = 1536
    return pl.pallas_call(
        _mod_kernel,
        out_shape=jax.ShapeDtypeStruct((n_layers, rows, w6), F32),
        grid=(n_layers, w6 // tn),
        in_specs=[
            pl.BlockSpec((rows, d), lambda l, j: (0, 0)),
            pl.BlockSpec((None, d, tn), lambda l, j: (l, 0, j)),
            pl.BlockSpec((None, 1, tn), lambda l, j: (l, 0, j)),
        ],
        out_specs=pl.BlockSpec((None, rows, tn), lambda l, j: (l, 0, j)),
        compiler_params=_cparams(("parallel", "parallel")),
        name="adaln_mod",
    )(cv, w_mod, b_mod.reshape(n_layers, 1, w6))


def _inproj_kernel(x_ref, mod_ref, cos_ref, sin_ref, g1_ref, win_ref, qng_ref, wqa_ref, wqb_ref,
                   kvg_ref, wka_ref, wkb_ref, wv_ref, wc_ref, wg_ref, bg_ref,
                   q_out, k_out, v_out, pq_out, gq_out, gk_out, gv_out, lfb_out, og_out, gate_out):
    d = D_MODEL
    x = x_ref[...]
    m = mod_ref[...]
    h = (_rms(x, g1_ref[...]) * (1.0 + m[:, d:2 * d]) + m[:, 0:d]).astype(BF16)

    def proj(a, b):
        return _dot(h, win_ref[:, a:b])

    cos = jnp.tile(cos_ref[...], (1, MLA_HEADS))
    sin = jnp.tile(sin_ref[...], (1, MLA_HEADS))

    nq = _rms(proj(C_Q, C_KV), qng_ref[...]).astype(BF16)
    q = (_dot(nq, wqa_ref[...]) * cos + _dot(nq, wqb_ref[...]) * sin) * MLA_SCALE
    q_out[...] = q.astype(BF16)
    ukv = proj(C_KV, C_FN)
    ckn = _rms(ukv[:, 0:MLA_KV_RANK], kvg_ref[...])
    lhs = jnp.concatenate([ckn, ukv[:, MLA_KV_RANK:]], axis=1).astype(BF16)
    k_out[...] = (_dot(lhs, wka_ref[...]) * cos + _dot(lhs, wkb_ref[...]) * sin).astype(BF16)
    v_out[...] = _dot(lhs, wv_ref[...]).astype(BF16)

    ufn = proj(C_FN, C_GQ).astype(BF16)
    ps, qs = [], []
    for g in range(FNET_GROUPS):
        r = _dot(ufn[:, g * FNET_GROUP_W:(g + 1) * FNET_GROUP_W], wc_ref[...])
        ps.append(r[:, 0:FNET_GROUP_W])
        qs.append(r[:, FNET_GROUP_W:])
    pq_out[...] = jnp.concatenate(ps + qs, axis=1).astype(BF16)

    gq_out[...] = (proj(C_GQ, C_GK) * GLA_SCALE).astype(BF16)
    gk_out[...] = proj(C_GK, C_GV).astype(BF16)
    gv_out[...] = proj(C_GV, C_OG).astype(BF16)
    og_out[...] = proj(C_OG, C_GG).astype(BF16)
    z = _dot(proj(C_GG, C_GATE).astype(BF16), wg_ref[...]) + bg_ref[...]
    lfb_out[...] = (jnp.minimum(z, 0.0) - jnp.log1p(jnp.exp(-jnp.abs(z)))) * (1.0 / GLA_TAU)
    gate_out[...] = proj(C_GATE, IN_WIDTH_PAD).astype(BF16)


def _inproj(xa, mod, cos_t, sin_t, lw, tm, n_lat_tiles):
    b, t_all, d = xa.shape
    nt = t_all // tm

    def tok(w, dt):
        return jax.ShapeDtypeStruct((b, t_all, w), dt)

    def tspec(w):
        return pl.BlockSpec((None, tm, w), lambda bi, i: (bi, i, 0))

    def cspec(a):
        return pl.BlockSpec(a.shape, lambda bi, i: (0,) * a.ndim)

    consts = [lw["norm1_g"], lw["w_in"], lw["q_norm_g"], lw["wqa"], lw["wqb"], lw["kv_norm_g"],
              lw["wka"], lw["wkb"], lw["wv"], lw["wc"], lw["wg"], lw["bg"]]
    widths = [(1024, BF16), (1024, BF16), (MLA_W, BF16), (2 * FNET_W, BF16), (GLA_QK_W, BF16),
              (GLA_QK_W, BF16), (GLA_W, BF16), (2 * GLA_QK_W, F32), (GLA_W, BF16), (3 * D_MODEL, BF16)]
    return pl.pallas_call(
        _inproj_kernel,
        out_shape=[tok(w, dt) for w, dt in widths],
        grid=(b, nt),
        in_specs=[
            tspec(d),
            pl.BlockSpec((None, None, 1, 6 * d), lambda bi, i: (bi, jnp.where(i < n_lat_tiles, 0, 1), 0, 0)),
            pl.BlockSpec((tm, HEAD_PAD), lambda bi, i: (i, 0)),
            pl.BlockSpec((tm, HEAD_PAD), lambda bi, i: (i, 0)),
        ] + [cspec(a) for a in consts],
        out_specs=[tspec(w) for w, _ in widths],
        compiler_params=_cparams(("parallel", "parallel")),
        name="inproj",
    )(xa, mod, cos_t, sin_t, *consts)


def _attn_heads(q_ref, k_ref, v_ref, o_ref, k0, nk):
    lane = lax.broadcasted_iota(I32, (q_ref.shape[0], 2 * MLA_V), 1)
    outs = []
    for hp in range(MLA_HEADS // 2):
        vp = v_ref[k0:k0 + nk, hp * 2 * MLA_V:(hp + 1) * 2 * MLA_V]
        rs = []
        for j in range(2):
            hd = 2 * hp + j
            q = q_ref[:, hd * HEAD_PAD:(hd + 1) * HEAD_PAD]
            k = k_ref[k0:k0 + nk, hd * HEAD_PAD:(hd + 1) * HEAD_PAD]
            s = _dot_nt(q, k)
            p = jnp.exp(s - jnp.max(s, axis=-1, keepdims=True))
            l = jnp.sum(p, axis=-1, keepdims=True)
            rs.append(_dot(p.astype(BF16), vp) / l)
        outs.append(jnp.where(lane < MLA_V, rs[0], rs[1]))
    o_ref[...] = jnp.concatenate(outs, axis=1).astype(BF16)


def _attn_kernel(q_ref, k_ref, v_ref, o_ref, *, n_lat_tiles, seq, ctx):
    i = pl.program_id(1)

    @pl.when(i < n_lat_tiles)
    def _():
        _attn_heads(q_ref, k_ref, v_ref, o_ref, 0, seq + ctx)

    @pl.when(i >= n_lat_tiles)
    def _():
        _attn_heads(q_ref, k_ref, v_ref, o_ref, seq, ctx)


def _attention(qc, kc, v, tq, seq, ctx):
    b, t_all, _ = qc.shape
    return pl.pallas_call(
        functools.partial(_attn_kernel, n_lat_tiles=seq // tq, seq=seq, ctx=ctx),
        out_shape=jax.ShapeDtypeStruct((b, t_all, MLA_W), BF16),
        grid=(b, t_all // tq),
        in_specs=[
            pl.BlockSpec((None, tq, MLA_HEADS * HEAD_PAD), lambda bi, i: (bi, i, 0)),
            pl.BlockSpec((None, t_all, MLA_HEADS * HEAD_PAD), lambda bi, i: (bi, 0, 0)),
            pl.BlockSpec((None, t_all, MLA_W), lambda bi, i: (bi, 0, 0)),
        ],
        out_specs=pl.BlockSpec((None, tq, MLA_W), lambda bi, i: (bi, i, 0)),
        compiler_params=_cparams(("parallel", "parallel")),
        name="mla_attention",
    )(qc, kc, v)


def _fourier_kernel(cm_ref, sm_ref, pq_ref, o_ref):
    y = _dot(cm_ref[...], pq_ref[:, 0:FNET_W]) + _dot(sm_ref[...], pq_ref[:, FNET_W:])
    o_ref[...] = y.astype(BF16)


def _fourier(pq, cm, sm, tm, row0):
    b = pq.shape[0]
    t = cm.shape[0]
    return pl.pallas_call(
        _fourier_kernel,
        out_shape=jax.ShapeDtypeStruct((b, t, FNET_W), BF16),
        grid=(t // tm, b),
        in_specs=[
            pl.BlockSpec((tm, t), lambda i, bi: (i, 0)),
            pl.BlockSpec((tm, t), lambda i, bi: (i, 0)),
            pl.BlockSpec((None, t, 2 * FNET_W), lambda i, bi: (bi, row0 // t, 0)),
        ],
        out_specs=pl.BlockSpec((None, tm, FNET_W), lambda i, bi: (bi, i, 0)),
        compiler_params=_cparams(("parallel", "parallel")),
        name="fourier_mix",
    )(cm, sm, pq)


def _gla_chunk(q, k, v, f, tri, ind_ref, mdiag_ref, st_ref, gs, qs, ks, a_s, reverse):
    c = GLA_CHUNK
    f0 = f.astype(BF16)
    r1 = f - f0.astype(F32)
    f1 = r1.astype(BF16)
    f2 = (r1 - f1.astype(F32)).astype(BF16)
    g = _dot(tri, f0) + _dot(tri, f1) + _dot(tri, f2)
    e = g[0:1, :] if reverse else g[c - 1:c, :]
    qf = q.astype(F32)
    kf = k.astype(F32)
    qg = (qf * jnp.exp(g)).astype(BF16)
    kd = (kf * jnp.exp(e - g)).astype(BF16)
    st = st_ref[...]
    st_b = st.astype(BF16)

    gs[...] = g
    qs[...] = qf
    ks[...] = kf
    row_id = lax.broadcasted_iota(I32, (c, GLA_QK_W), 0)

    def body(t, carry):
        gt = gs[pl.ds(t, 1), :]
        qt = qs[pl.ds(t, 1), :]
        w = jnp.exp(jnp.minimum(gt - gs[...], 0.0)) * ks[...] * qt
        keep = (row_id >= t) if reverse else (row_id <= t)
        w = jnp.where(keep, w, 0.0).astype(BF16)
        r = _dot(w, ind_ref[...])
        a_s[pl.ds(t, 1), :] = jnp.sum(r * mdiag_ref[...], axis=0, keepdims=True)
        return carry

    lax.fori_loop(0, c, body, 0)
    a = a_s[...].astype(BF16)

    outs, upd = [], []
    for hd in range(GLA_HEADS):
        ks_ = slice(hd * GLA_DK, (hd + 1) * GLA_DK)
        vh = v[:, hd * GLA_DV:(hd + 1) * GLA_DV]
        o_inter = _dot_nt(qg[:, ks_], st_b[:, ks_])
        o_intra = _dot(a[:, ks_], vh)
        outs.append(o_inter + o_intra)
        upd.append(_dot_tn(vh, kd[:, ks_]))
    st_ref[...] = st * jnp.exp(e) + jnp.concatenate(upd, axis=1)
    return jnp.concatenate(outs, axis=1)


def _gla_kernel(qf_ref, kf_ref, vf_ref, ff_ref, qb_ref, kb_ref, vb_ref, fb_ref,
                tril_ref, triu_ref, ind_ref, mdiag_ref, of_ref, ob_ref,
                stf, stb, gs, qs, ks, a_s, *, n_chunks):
    @pl.when(pl.program_id(1) == 0)
    def _():
        stf[...] = jnp.zeros_like(stf)
        stb[...] = jnp.zeros_like(stb)

    c = GLA_CHUNK
    for j in range(n_chunks):
        rows = slice(j * c, (j + 1) * c)
        of_ref[rows, :] = _gla_chunk(qf_ref[rows, :], kf_ref[rows, :], vf_ref[rows, :], ff_ref[rows, :],
                                     tril_ref[...], ind_ref, mdiag_ref, stf, gs, qs, ks, a_s, False)
    for j in reversed(range(n_chunks)):
        rows = slice(j * c, (j + 1) * c)
        ob_ref[rows, :] = _gla_chunk(qb_ref[rows, :], kb_ref[rows, :], vb_ref[rows, :], fb_ref[rows, :],
                                     triu_ref[...], ind_ref, mdiag_ref, stb, gs, qs, ks, a_s, True)


def _gla(gq, gk, gv, lfb, tb, seq, ctx):
    b, t_all, _ = gq.shape
    nlb, ncb = seq // tb, ctx // tb
    c = GLA_CHUNK

    def fwd(i):
        return jnp.where(i < ncb, nlb + i, i - ncb)

    def bwd(i):
        return jnp.where(i < ncb, nlb + ncb - 1 - i, nlb - 1 - (i - ncb))

    def spec(w, order, col=0):
        return pl.BlockSpec((None, tb, w), lambda bi, i: (bi, order(i), col))

    r = np.arange(c)
    tril = jnp.asarray(r[:, None] >= r[None, :], BF16)
    triu = jnp.asarray(r[:, None] <= r[None, :], BF16)
    hs = np.arange(GLA_QK_W)
    ind = jnp.asarray(hs[:, None] // GLA_DK == hs[None, :] // GLA_DK, BF16)
    mdiag = jnp.asarray(r[:, None] == hs[None, :] % GLA_DK, F32)

    def cspec(a):
        return pl.BlockSpec(a.shape, lambda bi, i: (0, 0))

    return pl.pallas_call(
        functools.partial(_gla_kernel, n_chunks=tb // c),
        out_shape=[jax.ShapeDtypeStruct((b, t_all, GLA_W), F32)] * 2,
        grid=(b, nlb + ncb),
        in_specs=[spec(GLA_QK_W, fwd), spec(GLA_QK_W, fwd), spec(GLA_W, fwd), spec(GLA_QK_W, fwd, 0),
                  spec(GLA_QK_W, bwd), spec(GLA_QK_W, bwd), spec(GLA_W, bwd), spec(GLA_QK_W, bwd, 1),
                  cspec(tril), cspec(triu), cspec(ind), cspec(mdiag)],
        out_specs=[spec(GLA_W, fwd), spec(GLA_W, bwd)],
        scratch_shapes=[pltpu.VMEM((GLA_DV, GLA_QK_W), F32), pltpu.VMEM((GLA_DV, GLA_QK_W), F32),
                        pltpu.VMEM((c, GLA_QK_W), F32), pltpu.VMEM((c, GLA_QK_W), F32),
                        pltpu.VMEM((c, GLA_QK_W), F32), pltpu.VMEM((c, GLA_QK_W), F32)],
        compiler_params=_cparams(("parallel", "arbitrary")),
        name="gla_scan",
    )(gq, gk, gv, lfb, gq, gk, gv, lfb, tril, triu, ind, mdiag)


def _merge_kernel(x_ref, mod_ref, om_ref, yfl_ref, yfc_ref, of_ref, ob_ref, og_ref, gate_ref,
                  gng_ref, wbm_ref, wbf_ref, wbg_ref, wo_ref, g2_ref, rw_ref, rb_ref,
                  x_out, h_out, idx_out, gw_out, *, n_lat_tiles):
    d = D_MODEL
    m = mod_ref[...]
    y_fn = jnp.where(pl.program_id(1) < n_lat_tiles, yfl_ref[...], yfc_ref[...])
    o = of_ref[...] + ob_ref[...]
    gng = gng_ref[...]
    og = og_ref[...].astype(F32)
    parts = []
    for hd in range(GLA_HEADS):
        sl = slice(hd * GLA_DV, (hd + 1) * GLA_DV)
        parts.append(_rms(o[:, sl], gng[:, sl]))
    o_gla = (jnp.concatenate(parts, axis=1) * (og * jax.nn.sigmoid(og))).astype(BF16)

    gate = jax.nn.sigmoid(gate_ref[...].astype(F32))
    y = (gate[:, 0:d] * _dot(om_ref[...], wbm_ref[...])
         + gate[:, d:2 * d] * _dot(y_fn, wbf_ref[...])
         + gate[:, 2 * d:3 * d] * _dot(o_gla, wbg_ref[...]))
    x1 = x_ref[...] + m[:, 2 * d:3 * d] * _dot(y.astype(BF16), wo_ref[...])
    x_out[...] = x1
    h2 = _rms(x1, g2_ref[...]) * (1.0 + m[:, 4 * d:5 * d]) + m[:, 3 * d:4 * d]
    h_out[...] = h2

    logits = _dot(h2.astype(BF16), rw_ref[...]) + rb_ref[...]
    lane = lax.broadcasted_iota(I32, logits.shape, 1)
    lane_f = lane.astype(F32)
    neg = jnp.float32(-jnp.inf)
    lg = jnp.where(lane < N_EXPERTS, logits, neg)
    idx_acc = jnp.zeros(logits.shape, I32)
    val_acc = jnp.zeros(logits.shape, F32)
    v0 = None
    for kk in range(TOP_K):
        mx = jnp.max(lg, axis=-1, keepdims=True)
        ix = jnp.min(jnp.where(lg == mx, lane_f, float(ROUTER_PAD)), axis=-1, keepdims=True).astype(I32)
        if kk == 0:
            v0 = mx
        idx_acc = jnp.where(lane == kk, ix, idx_acc)
        val_acc = jnp.where(lane == kk, jnp.exp(mx - v0), val_acc)
        lg = jnp.where(lane == ix, neg, lg)
    idx_out[...] = idx_acc
    gw_out[...] = val_acc / jnp.sum(val_acc, axis=-1, keepdims=True)


def _merge(xa, mod, o_mla, y_fn_lat, y_fn_ctx, o_f, o_b, og, gate, lw, tm, n_lat_tiles):
    b, t_all, d = xa.shape

    def tspec(w):
        return pl.BlockSpec((None, tm, w), lambda bi, i: (bi, i, 0))

    yfl_spec = pl.BlockSpec((None, tm, FNET_W), lambda bi, i: (bi, jnp.minimum(i, n_lat_tiles - 1), 0))
    yfc_spec = pl.BlockSpec((None, tm, FNET_W), lambda bi, i: (bi, jnp.maximum(i - n_lat_tiles, 0), 0))

    def cspec(a):
        return pl.BlockSpec(a.shape, lambda bi, i: (0,) * a.ndim)

    consts = [lw["gla_norm_g"], lw["w_br_mla"], lw["w_br_fnet"], lw["w_br_gla"], lw["w_o"],
              lw["norm2_g"], lw["router_w"], lw["router_b"]]
    return pl.pallas_call(
        functools.partial(_merge_kernel, n_lat_tiles=n_lat_tiles),
        out_shape=[jax.ShapeDtypeStruct((b, t_all, d), F32), jax.ShapeDtypeStruct((b, t_all, d), F32),
                   jax.ShapeDtypeStruct((b, t_all, ROUTER_PAD), I32),
                   jax.ShapeDtypeStruct((b, t_all, ROUTER_PAD), F32)],
        grid=(b, t_all // tm),
        in_specs=[
            tspec(d),
            pl.BlockSpec((None, None, 1, 6 * d), lambda bi, i: (bi, jnp.where(i < n_lat_tiles, 0, 1), 0, 0)),
            tspec(MLA_W), yfl_spec, yfc_spec, tspec(GLA_W), tspec(GLA_W), tspec(GLA_W), tspec(3 * d),
        ] + [cspec(a) for a in consts],
        out_specs=[tspec(d), tspec(d), tspec(ROUTER_PAD), tspec(ROUTER_PAD)],
        input_output_aliases={0: 0},
        compiler_params=_cparams(("parallel", "parallel")),
        name="merge_router",
    )(xa, mod, o_mla, y_fn_lat, y_fn_ctx, o_f, o_b, og, gate, *consts)


def _row_gather(idx_smem, slot, src_hbm, dst, sem, n_rows):
    def body(r, carry):
        pltpu.make_async_copy(src_hbm.at[pl.ds(idx_smem[slot, r], 1)], dst.at[slot, pl.ds(r, 1)],
                              sem.at[slot]).start()
        return carry

    lax.fori_loop(0, n_rows, body, 0)


def _moe_kernel(blk_e_ref, tok_hbm, h_hbm, gw_ref, wu_ref, bu_ref, wd_ref, bd_ref, y_out,
                idx_smem, xbuf, isem, gsem):
    i = pl.program_id(0)
    n = pl.num_programs(0)
    slot = lax.rem(i, 2)
    nxt = 1 - slot

    def idx_copy(blk, s):
        return pltpu.make_async_copy(tok_hbm.at[blk], idx_smem.at[s], isem.at[s])

    @pl.when(i == 0)
    def _():
        idx_copy(0, 0).start()
        idx_copy(0, 0).wait()
        _row_gather(idx_smem, 0, h_hbm, xbuf, gsem, MOE_BLOCK)

        @pl.when(n > 1)
        def _():
            idx_copy(1, 1).start()

    @pl.when(i + 1 < n)
    def _():
        idx_copy(i + 1, nxt).wait()
        _row_gather(idx_smem, nxt, h_hbm, xbuf, gsem, MOE_BLOCK)

    @pl.when(i + 2 < n)
    def _():
        idx_copy(i + 2, slot).start()

    pltpu.make_async_copy(xbuf.at[slot], xbuf.at[slot], gsem.at[slot]).wait()
    xb = xbuf[slot].astype(BF16)
    up = _dot(xb, wu_ref[...]) + bu_ref[...]
    glu = jnp.minimum(up[:, 0:D_EXPERT], SWIGLU_LIMIT)
    lin = jnp.clip(up[:, D_EXPERT:], -SWIGLU_LIMIT, SWIGLU_LIMIT)
    act = glu * jax.nn.sigmoid(SWIGLU_ALPHA * glu) * (lin + 1.0)
    yb = _dot(act.astype(BF16), wd_ref[...]) + bd_ref[...]
    y_out[...] = yb * gw_ref[...]


def _moe_experts(blk_e, row_tok, row_gate, h_flat, lw):
    n_blocks = blk_e.shape[0]
    d = D_MODEL
    grid_spec = pltpu.PrefetchScalarGridSpec(
        num_scalar_prefetch=1,
        grid=(n_blocks,),
        in_specs=[
            pl.BlockSpec(memory_space=pl.ANY),
            pl.BlockSpec(memory_space=pl.ANY),
            pl.BlockSpec((MOE_BLOCK, 1), lambda i, be: (i, 0)),
            pl.BlockSpec((None, d, 2 * D_EXPERT), lambda i, be: (be[i], 0, 0)),
            pl.BlockSpec((None, 1, 2 * D_EXPERT), lambda i, be: (be[i], 0, 0)),
            pl.BlockSpec((None, D_EXPERT, d), lambda i, be: (be[i], 0, 0)),
            pl.BlockSpec((None, 1, d), lambda i, be: (be[i], 0, 0)),
        ],
        out_specs=pl.BlockSpec((MOE_BLOCK, d), lambda i, be: (i, 0)),
        scratch_shapes=[pltpu.SMEM((2, MOE_BLOCK), I32), pltpu.VMEM((2, MOE_BLOCK, d), F32),
                        pltpu.SemaphoreType.DMA((2,)), pltpu.SemaphoreType.DMA((2,))],
    )
    return pl.pallas_call(
        _moe_kernel,
        out_shape=jax.ShapeDtypeStruct((n_blocks * MOE_BLOCK, d), F32),
        grid_spec=grid_spec,
        compiler_params=_cparams(("arbitrary",)),
        name="moe_experts",
    )(blk_e, row_tok, h_flat, row_gate, lw["w_up"], lw["b_up"], lw["w_down"], lw["b_down"])


def _combine_kernel(pos_hbm, y_hbm, x_ref, mod_ref, x_out, idx_smem, ybuf, isem, gsem, *, tm, n_tiles):
    bi = pl.program_id(0)
    ti = pl.program_id(1)
    i = bi * n_tiles + ti
    n = pl.num_programs(0) * n_tiles
    slot = lax.rem(i, 2)
    nxt = 1 - slot
    n_rows = tm * TOP_K

    def idx_copy(blk, s):
        return pltpu.make_async_copy(pos_hbm.at[blk], idx_smem.at[s], isem.at[s])

    @pl.when(i == 0)
    def _():
        idx_copy(0, 0).start()
        idx_copy(0, 0).wait()
        _row_gather(idx_smem, 0, y_hbm, ybuf, gsem, n_rows)

        @pl.when(n > 1)
        def _():
            idx_copy(1, 1).start()

    @pl.when(i + 1 < n)
    def _():
        idx_copy(i + 1, nxt).wait()
        _row_gather(idx_smem, nxt, y_hbm, ybuf, gsem, n_rows)

    @pl.when(i + 2 < n)
    def _():
        idx_copy(i + 2, slot).start()

    pltpu.make_async_copy(ybuf.at[slot], ybuf.at[slot], gsem.at[slot]).wait()
    acc = ybuf[slot, 0:tm, :]
    for kk in range(1, TOP_K):
        acc = acc + ybuf[slot, kk * tm:(kk + 1) * tm, :]
    d = D_MODEL
    x_out[...] = x_ref[...] + mod_ref[:, 5 * d:6 * d] * acc


def _combine(xa, mod, pos, y_sorted, tm, n_lat_tiles):
    b, t_all, d = xa.shape
    n_tiles = t_all // tm
    return pl.pallas_call(
        functools.partial(_combine_kernel, tm=tm, n_tiles=n_tiles),
        out_shape=jax.ShapeDtypeStruct((b, t_all, d), F32),
        grid=(b, n_tiles),
        in_specs=[
            pl.BlockSpec(memory_space=pl.ANY),
            pl.BlockSpec(memory_space=pl.ANY),
            pl.BlockSpec((None, tm, d), lambda bi, i: (bi, i, 0)),
            pl.BlockSpec((None, None, 1, 6 * d), lambda bi, i: (bi, jnp.where(i < n_lat_tiles, 0, 1), 0, 0)),
        ],
        out_specs=pl.BlockSpec((None, tm, d), lambda bi, i: (bi, i, 0)),
        scratch_shapes=[pltpu.SMEM((2, tm * TOP_K), I32), pltpu.VMEM((2, tm * TOP_K, d), F32),
                        pltpu.SemaphoreType.DMA((2,)), pltpu.SemaphoreType.DMA((2,))],
        input_output_aliases={2: 0},
        compiler_params=_cparams(("arbitrary", "arbitrary")),
        name="moe_combine",
    )(pos, y_sorted, xa, mod)


def _final_kernel(x_ref, g_ref, o_ref):
    o_ref[...] = _rms(x_ref[...], g_ref[...])


def _final_norm(xa, g, seq, tm):
    b, _, d = xa.shape
    return pl.pallas_call(
        _final_kernel,
        out_shape=jax.ShapeDtypeStruct((b, seq, d), F32),
        grid=(b, seq // tm),
        in_specs=[pl.BlockSpec((None, tm, d), lambda bi, i: (bi, i, 0)),
                  pl.BlockSpec((1, d), lambda bi, i: (0, 0))],
        out_specs=pl.BlockSpec((None, tm, d), lambda bi, i: (bi, i, 0)),
        compiler_params=_cparams(("parallel", "parallel")),
        name="final_norm",
    )(xa, g.reshape(1, d))


def _pad_cols(w, width):
    return jnp.pad(w, ((0, 0), (0, width - w.shape[1])))


def _rope_partner(w):
    a = ROPE_AXIS // 2
    return jnp.concatenate([-w[:, a:2 * a], w[:, 0:a], -w[:, 3 * a:4 * a], w[:, 2 * a:3 * a]], axis=1)


def _layer_weights(l, w_in, mla_q_norm_g, mla_w_uq, mla_kv_norm_g, mla_w_ukv, gla_w_gate_f, gla_b_gate_f,
                   gla_w_gate_b, gla_b_gate_b, gla_norm_g, w_br_mla, w_br_fnet, w_br_gla, w_o, norm1_g,
                   norm2_g, router_w, router_b, exp_w_up, exp_b_up, exp_w_down, exp_b_down):
    d = D_MODEL
    splits = np.cumsum([MLA_Q_RANK, MLA_KV_RANK + MLA_ROPE, FNET_W, GLA_QK_W, GLA_QK_W, GLA_W, GLA_W,
                        GLA_GATE_RANK, GLA_GATE_RANK])
    wq, wkv, wfn, wgq, wgk, wgv, wog, wgf, wgb, wgate = jnp.split(w_in[l], [int(s) for s in splits], axis=1)
    w_in_p = jnp.concatenate([wq, _pad_cols(wkv, 256), wfn, wgq, wgk, wgv, wog,
                              _pad_cols(jnp.concatenate([wgf, wgb], axis=1), 128), wgate], axis=1).astype(BF16)

    qk = MLA_NOPE + MLA_ROPE
    zq = jnp.zeros((MLA_Q_RANK, HEAD_PAD - qk), F32)
    zn = jnp.zeros((MLA_Q_RANK, MLA_NOPE), F32)
    wqa, wqb = [], []
    for hd in range(MLA_HEADS):
        wh = mla_w_uq[l][:, hd * qk:(hd + 1) * qk]
        wqa.append(jnp.concatenate([wh, zq], axis=1))
        wqb.append(jnp.concatenate([zn, _rope_partner(wh[:, MLA_NOPE:]), zq], axis=1))
    eye = jnp.eye(MLA_ROPE, dtype=F32)
    wka, wkb, wv = [], [], []
    for hd in range(MLA_HEADS):
        wh = mla_w_ukv[l][:, hd * (MLA_NOPE + MLA_V):(hd + 1) * (MLA_NOPE + MLA_V)]
        top = jnp.concatenate([wh[:, 0:MLA_NOPE], jnp.zeros((MLA_KV_RANK, HEAD_PAD - MLA_NOPE), F32)], axis=1)
        mid_a = jnp.concatenate([jnp.zeros((MLA_ROPE, MLA_NOPE), F32), eye,
                                 jnp.zeros((MLA_ROPE, HEAD_PAD - qk), F32)], axis=1)
        mid_b = jnp.concatenate([jnp.zeros((MLA_ROPE, MLA_NOPE), F32), _rope_partner(eye),
                                 jnp.zeros((MLA_ROPE, HEAD_PAD - qk), F32)], axis=1)
        bot = jnp.zeros((256 - MLA_KV_RANK - MLA_ROPE, HEAD_PAD), F32)
        wka.append(jnp.concatenate([top, mid_a, bot], axis=0))
        wkb.append(jnp.concatenate([jnp.zeros_like(top), mid_b, bot], axis=0))
        wv.append(jnp.concatenate([wh[:, MLA_NOPE:], jnp.zeros((256 - MLA_KV_RANK, MLA_V), F32)], axis=0))

    ch = np.arange(FNET_GROUP_W)
    ang = 2.0 * np.pi * ((ch[:, None] * ch[None, :]) % FNET_GROUP_W) / FNET_GROUP_W
    wc = np.concatenate([np.cos(ang), np.sin(ang)], axis=1) / math.sqrt(FNET_GROUP_W)

    wg = jnp.zeros((128, 2 * GLA_QK_W), F32)
    wg = wg.at[0:GLA_GATE_RANK, 0:GLA_QK_W].set(gla_w_gate_f[l])
    wg = wg.at[GLA_GATE_RANK:2 * GLA_GATE_RANK, GLA_QK_W:].set(gla_w_gate_b[l])

    return {
        "norm1_g": norm1_g[l].reshape(1, d),
        "w_in": w_in_p,
        "q_norm_g": mla_q_norm_g[l].reshape(1, -1),
        "wqa": jnp.concatenate(wqa, axis=1).astype(BF16),
        "wqb": jnp.concatenate(wqb, axis=1).astype(BF16),
        "kv_norm_g": mla_kv_norm_g[l].reshape(1, -1),
        "wka": jnp.concatenate(wka, axis=1).astype(BF16),
        "wkb": jnp.concatenate(wkb, axis=1).astype(BF16),
        "wv": jnp.concatenate(wv, axis=1).astype(BF16),
        "wc": jnp.asarray(wc, BF16),
        "wg": wg.astype(BF16),
        "bg": jnp.concatenate([gla_b_gate_f[l], gla_b_gate_b[l]]).reshape(1, -1),
        "gla_norm_g": gla_norm_g[l].reshape(1, -1),
        "w_br_mla": w_br_mla[l].astype(BF16),
        "w_br_fnet": w_br_fnet[l].astype(BF16),
        "w_br_gla": w_br_gla[l].astype(BF16),
        "w_o": w_o[l].astype(BF16),
        "norm2_g": norm2_g[l].reshape(1, d),
        "router_w": _pad_cols(router_w[l], ROUTER_PAD).astype(BF16),
        "router_b": _pad_cols(router_b[l].reshape(1, -1), ROUTER_PAD),
        "w_up": exp_w_up[l].astype(BF16),
        "b_up": exp_b_up[l].reshape(N_EXPERTS, 1, -1),
        "w_down": exp_w_down[l].astype(BF16),
        "b_down": exp_b_down[l].reshape(N_EXPERTS, 1, -1),
    }


def _rope_tables(seq, ctx):
    rows = seq // GRID_W
    row = jnp.repeat(jnp.arange(rows, dtype=F32), GRID_W)
    col = jnp.tile(jnp.arange(GRID_W, dtype=F32), rows)
    inv_freq = ROPE_BASE ** (-jnp.arange(0, ROPE_AXIS, 2, dtype=F32) / ROPE_AXIS)
    ang_r = row[:, None] * inv_freq
    ang_c = col[:, None] * inv_freq
    ang = jnp.concatenate([ang_r, ang_r, ang_c, ang_c], axis=1)
    ang = jnp.concatenate([ang, jnp.zeros((ctx, MLA_ROPE), F32)], axis=0)
    t_all = seq + ctx
    pad = jnp.zeros((t_all, HEAD_PAD - MLA_NOPE - MLA_ROPE), F32)
    cos_t = jnp.concatenate([jnp.ones((t_all, MLA_NOPE), F32), jnp.cos(ang), pad], axis=1)
    sin_t = jnp.concatenate([jnp.zeros((t_all, MLA_NOPE), F32), jnp.sin(ang), pad], axis=1)
    return cos_t, sin_t


def _dft_mats(t):
    r = np.arange(t, dtype=np.int64)
    ang = 2.0 * np.pi * ((r[:, None] * r[None, :]) % t) / t
    s = 1.0 / math.sqrt(t)
    return jnp.asarray(np.cos(ang) * s, BF16), jnp.asarray(-np.sin(ang) * s, BF16)


def _routing(top_idx, gates, n_tok):
    n_assign = n_tok * TOP_K
    flat_e = top_idx.reshape(n_assign)
    onehot = (flat_e[:, None] == jnp.arange(N_EXPERTS, dtype=I32)[None, :]).astype(I32)
    csum = jnp.cumsum(onehot, axis=0)
    counts = csum[-1]
    rank = jnp.sum(csum * onehot, axis=1) - 1
    padded = ((counts + MOE_BLOCK - 1) // MOE_BLOCK) * MOE_BLOCK
    pad_end = jnp.cumsum(padded)
    pad_start = pad_end - padded
    dest = (pad_start[flat_e] + rank).astype(I32)
    n_blocks = -(-n_assign // MOE_BLOCK) + N_EXPERTS
    n_rows = n_blocks * MOE_BLOCK
    row_tok = jnp.zeros((n_rows,), I32).at[dest].set(jnp.arange(n_assign, dtype=I32) // TOP_K)
    row_gate = jnp.zeros((n_rows,), F32).at[dest].set(gates.reshape(n_assign))
    blk_e = jnp.minimum(jnp.searchsorted(pad_end, jnp.arange(n_blocks, dtype=I32) * MOE_BLOCK, side="right"),
                        N_EXPERTS - 1).astype(I32)
    return blk_e, row_tok.reshape(n_blocks, MOE_BLOCK), row_gate.reshape(n_rows, 1), dest


def kernel(x, c, ctx, c_ctx, w_mod, b_mod, norm1_g, w_in, mla_q_norm_g, mla_w_uq, mla_kv_norm_g, mla_w_ukv,
           gla_w_gate_f, gla_b_gate_f, gla_w_gate_b, gla_b_gate_b, gla_norm_g, w_br_mla, w_br_fnet, w_br_gla,
           w_o, norm2_g, router_w, router_b, exp_w_up, exp_b_up, exp_w_down, exp_b_down, final_norm_g):
    b, seq, d = x.shape
    n_ctx = ctx.shape[1]
    n_layers = w_mod.shape[0]
    t_all = seq + n_ctx
    assert d == D_MODEL and seq % n_ctx == 0 and seq % GRID_W == 0 and n_ctx % GLA_CHUNK == 0
    tm = min(256, n_ctx)
    assert n_ctx % tm == 0 and seq % tm == 0 and (tm * TOP_K) % MOE_BLOCK == 0
    n_lat_tiles = seq // tm
    t_fourier = min(512, seq)

    xa = jnp.concatenate([x, ctx], axis=1)

    rows = -(-(b + 1) // 8) * 8
    cv = jnp.zeros((rows, d), F32).at[0:b].set(c).at[b].set(c_ctx)
    mod_all = _mod_all(cv, w_mod, b_mod)
    cos_t, sin_t = _rope_tables(seq, n_ctx)
    cm_l, sm_l = _dft_mats(seq)
    cm_c, sm_c = _dft_mats(n_ctx)

    n_tok = b * t_all
    for l in range(n_layers):
        lw = _layer_weights(l, w_in, mla_q_norm_g, mla_w_uq, mla_kv_norm_g, mla_w_ukv, gla_w_gate_f,
                            gla_b_gate_f, gla_w_gate_b, gla_b_gate_b, gla_norm_g, w_br_mla, w_br_fnet,
                            w_br_gla, w_o, norm1_g, norm2_g, router_w, router_b, exp_w_up, exp_b_up,
                            exp_w_down, exp_b_down)
        mod_lat = mod_all[l, 0:b]
        mod_ctx = jnp.broadcast_to(mod_all[l, b][None, :], (b, 6 * d))
        mod = jnp.stack([mod_lat, mod_ctx], axis=1).reshape(b, 2, 1, 6 * d)

        qc, kc, v, pq, gq, gk, gv, lfb, og, gate = _inproj(xa, mod, cos_t, sin_t, lw, tm, n_lat_tiles)
        o_mla = _attention(qc, kc, v, tm, seq, n_ctx)
        y_fn_lat = _fourier(pq, cm_l, sm_l, t_fourier, 0)
        y_fn_ctx = _fourier(pq, cm_c, sm_c, n_ctx, seq)
        o_f, o_b = _gla(gq, gk, gv, lfb, tm, seq, n_ctx)
        xa, h2, top_idx, gates = _merge(xa, mod, o_mla, y_fn_lat, y_fn_ctx, o_f, o_b, og, gate, lw, tm,
                                        n_lat_tiles)

        blk_e, row_tok, row_gate, dest = _routing(top_idx[..., 0:TOP_K], gates[..., 0:TOP_K], n_tok)
        y_sorted = _moe_experts(blk_e, row_tok, row_gate, h2.reshape(n_tok, d), lw)
        pos = dest.reshape(n_tok // tm, tm, TOP_K).transpose(0, 2, 1).reshape(n_tok // tm, tm * TOP_K)
        xa = _combine(xa, mod, pos, y_sorted, tm, n_lat_tiles)

    return _final_norm(xa, final_norm_g, seq, tm)
```

```python
import functools
import math

import jax
import jax.numpy as jnp
import numpy as np
from jax import lax
from jax.experimental import pallas as pl
from jax.experimental.pallas import tpu as pltpu

F32 = jnp.float32
BF16 = jnp.bfloat16
I32 = jnp.int32
U32 = jnp.uint32

D_MODEL = 1024
GRID_W = 64
EPS = 1e-6

MLA_HEADS = 8
MLA_Q_RANK = 256
MLA_KV_RANK = 128
MLA_NOPE = 64
MLA_ROPE = 32
MLA_V = 64
MLA_SCALE = (MLA_NOPE + MLA_ROPE) ** -0.5
ROPE_AXIS = MLA_ROPE // 2
ROPE_BASE = 10000.0
HEAD_PAD = 128

FNET_GROUPS = 4
FNET_GROUP_W = 128
FNET_W = FNET_GROUPS * FNET_GROUP_W

GLA_HEADS = 4
GLA_DK = 64
GLA_DV = 128
GLA_GATE_RANK = 16
GLA_TAU = 16.0
GLA_CHUNK = 64
GLA_SUB = 16
GLA_SCALE = GLA_DK ** -0.5
GLA_QK_W = GLA_HEADS * GLA_DK
GLA_W = GLA_HEADS * GLA_DV

N_EXPERTS = 32
TOP_K = 4
D_EXPERT = D_MODEL
SWIGLU_ALPHA = 1.702
SWIGLU_LIMIT = 7.0
MOE_BLOCK = 256
ROUTER_PAD = 128

MLA_W = MLA_HEADS * MLA_V

VMEM_LIMIT_V7X = 56 * 1024 * 1024

C_Q = 0
C_KV = C_Q + MLA_Q_RANK
C_FN = C_KV + 256
C_GQ = C_FN + FNET_W
C_GK = C_GQ + GLA_QK_W
C_GV = C_GK + GLA_QK_W
C_OG = C_GV + GLA_W
C_GG = C_OG + GLA_W
C_GATE = C_GG + 128
IN_WIDTH_PAD = C_GATE + 3 * D_MODEL


def _cparams(sem, vmem=VMEM_LIMIT_V7X):
    return pltpu.CompilerParams(dimension_semantics=sem, vmem_limit_bytes=vmem)


def _rms(x, g):
    return x * lax.rsqrt(jnp.mean(x * x, axis=-1, keepdims=True) + EPS) * g


def _dot(a, b):
    return jnp.dot(a, b, preferred_element_type=F32)


def _dot_nt(a, b):
    return lax.dot_general(a, b, (((1,), (1,)), ((), ())), preferred_element_type=F32)


def _dot_tn(a, b):
    return lax.dot_general(a, b, (((0,), (0,)), ((), ())), preferred_element_type=F32)


def _pack_bf16_pairs(x):
    w = x.shape[1] // 2
    r = lax.bitcast_convert_type(x.astype(BF16).astype(F32), U32)
    return r[:, w:] | (r[:, :w] >> 16)


def _unpack_bf16_pairs(u):
    lo = lax.bitcast_convert_type(u << 16, F32)
    hi = lax.bitcast_convert_type(u & jnp.uint32(0xFFFF0000), F32)
    return lo, hi


def _mod_kernel(cv_ref, w_ref, b_ref, o_ref):
    cv = cv_ref[...]
    s = cv * jax.nn.sigmoid(cv)
    o_ref[...] = jnp.dot(s, w_ref[...], preferred_element_type=F32,
                         precision=lax.Precision.HIGHEST) + b_ref[...]


def _mod_all(cv, w_mod, b_mod):
    n_layers, d, w6 = w_mod.shape
    rows = cv.shape[0]
    tn = 1536
    return pl.pallas_call(
        _mod_kernel,
        out_shape=jax.ShapeDtypeStruct((n_layers, rows, w6), F32),
        grid=(n_layers, w6 // tn),
        in_specs=[
            pl.BlockSpec((rows, d), lambda l, j: (0, 0)),
            pl.BlockSpec((None, d, tn), lambda l, j: (l, 0, j)),
            pl.BlockSpec((None, 1, tn), lambda l, j: (l, 0, j)),
        ],
        out_specs=pl.BlockSpec((None, rows, tn), lambda l, j: (l, 0, j)),
        compiler_params=_cparams(("parallel", "parallel")),
        name="adaln_mod",
    )(cv, w_mod, b_mod.reshape(n_layers, 1, w6))


def _inproj_kernel(x_ref, mod_ref, cos_ref, sin_ref, g1_ref, win_ref, qng_ref, wqa_ref, wqb_ref,
                   kvg_ref, wka_ref, wkb_ref, wv_ref, wc_ref, wg_ref, bg_ref,
                   q_out, k_out, v_out, pq_out, gq_out, gk_out, gv_out, lfb_out, og_out, gate_out):
    d = D_MODEL
    x = x_ref[...]
    m = mod_ref[...]
    h = (_rms(x, g1_ref[...]) * (1.0 + m[:, d:2 * d]) + m[:, 0:d]).astype(BF16)

    def proj(a, b):
        return _dot(h, win_ref[:, a:b])

    cos = jnp.tile(cos_ref[...], (1, MLA_HEADS))
    sin = jnp.tile(sin_ref[...], (1, MLA_HEADS))

    nq = _rms(proj(C_Q, C_KV), qng_ref[...]).astype(BF16)
    q = (_dot(nq, wqa_ref[...]) * cos + _dot(nq, wqb_ref[...]) * sin) * MLA_SCALE
    q_out[...] = q.astype(BF16)
    ukv = proj(C_KV, C_FN)
    ckn = _rms(ukv[:, 0:MLA_KV_RANK], kvg_ref[...])
    lhs = jnp.concatenate([ckn, ukv[:, MLA_KV_RANK:]], axis=1).astype(BF16)
    k_out[...] = (_dot(lhs, wka_ref[...]) * cos + _dot(lhs, wkb_ref[...]) * sin).astype(BF16)
    v_out[...] = _dot(lhs, wv_ref[...]).astype(BF16)

    ufn = proj(C_FN, C_GQ).astype(BF16)
    ps, qs = [], []
    for g in range(FNET_GROUPS):
        r = _dot(ufn[:, g * FNET_GROUP_W:(g + 1) * FNET_GROUP_W], wc_ref[...])
        ps.append(r[:, 0:FNET_GROUP_W])
        qs.append(r[:, FNET_GROUP_W:])
    pq_out[...] = jnp.concatenate(ps + qs, axis=1).astype(BF16)

    gq_out[...] = (proj(C_GQ, C_GK) * GLA_SCALE).astype(BF16)
    gk_out[...] = proj(C_GK, C_GV).astype(BF16)
    gv_out[...] = proj(C_GV, C_OG).astype(BF16)
    og_out[...] = proj(C_OG, C_GG).astype(BF16)
    z = _dot(proj(C_GG, C_GATE).astype(BF16), wg_ref[...]) + bg_ref[...]
    lfb_out[...] = (jnp.minimum(z, 0.0) - jnp.log1p(jnp.exp(-jnp.abs(z)))) * (1.0 / GLA_TAU)
    gate_out[...] = proj(C_GATE, IN_WIDTH_PAD).astype(BF16)


def _inproj(xa, mod, cos_t, sin_t, lw, tm, n_lat_tiles):
    b, t_all, d = xa.shape
    nt = t_all // tm

    def tok(w, dt):
        return jax.ShapeDtypeStruct((b, t_all, w), dt)

    def tspec(w):
        return pl.BlockSpec((None, tm, w), lambda bi, i: (bi, i, 0))

    def cspec(a):
        return pl.BlockSpec(a.shape, lambda bi, i: (0,) * a.ndim)

    consts = [lw["norm1_g"], lw["w_in"], lw["q_norm_g"], lw["wqa"], lw["wqb"], lw["kv_norm_g"],
              lw["wka"], lw["wkb"], lw["wv"], lw["wc"], lw["wg"], lw["bg"]]
    widths = [(1024, BF16), (1024, BF16), (MLA_W, BF16), (2 * FNET_W, BF16), (GLA_QK_W, BF16),
              (GLA_QK_W, BF16), (GLA_W, BF16), (2 * GLA_QK_W, F32), (GLA_W, BF16), (3 * D_MODEL, BF16)]
    return pl.pallas_call(
        _inproj_kernel,
        out_shape=[tok(w, dt) for w, dt in widths],
        grid=(b, nt),
        in_specs=[
            tspec(d),
            pl.BlockSpec((None, None, 1, 6 * d), lambda bi, i: (bi, jnp.where(i < n_lat_tiles, 0, 1), 0, 0)),
            pl.BlockSpec((tm, HEAD_PAD), lambda bi, i: (i, 0)),
            pl.BlockSpec((tm, HEAD_PAD), lambda bi, i: (i, 0)),
        ] + [cspec(a) for a in consts],
        out_specs=[tspec(w) for w, _ in widths],
        compiler_params=_cparams(("parallel", "parallel")),
        name="inproj",
    )(xa, mod, cos_t, sin_t, *consts)


def _attn_heads(q_ref, k_ref, v_ref, o_ref, k0, nk):
    lane = lax.broadcasted_iota(I32, (q_ref.shape[0], 2 * MLA_V), 1)
    outs = []
    for hp in range(MLA_HEADS // 2):
        vp = v_ref[k0:k0 + nk, hp * 2 * MLA_V:(hp + 1) * 2 * MLA_V]
        rs = []
        for j in range(2):
            hd = 2 * hp + j
            q = q_ref[:, hd * HEAD_PAD:(hd + 1) * HEAD_PAD]
            k = k_ref[k0:k0 + nk, hd * HEAD_PAD:(hd + 1) * HEAD_PAD]
            s = _dot_nt(q, k)
            p = jnp.exp(s - jnp.max(s, axis=-1, keepdims=True))
            l = jnp.sum(p, axis=-1, keepdims=True)
            rs.append(_dot(p.astype(BF16), vp) / l)
        outs.append(jnp.where(lane < MLA_V, rs[0], rs[1]))
    o_ref[...] = jnp.concatenate(outs, axis=1).astype(BF16)


def _attn_kernel(q_ref, k_ref, v_ref, o_ref, *, n_lat_tiles, seq, ctx):
    i = pl.program_id(1)

    @pl.when(i < n_lat_tiles)
    def _():
        _attn_heads(q_ref, k_ref, v_ref, o_ref, 0, seq + ctx)

    @pl.when(i >= n_lat_tiles)
    def _():
        _attn_heads(q_ref, k_ref, v_ref, o_ref, seq, ctx)


def _attention(qc, kc, v, tq, seq, ctx):
    b, t_all, _ = qc.shape
    return pl.pallas_call(
        functools.partial(_attn_kernel, n_lat_tiles=seq // tq, seq=seq, ctx=ctx),
        out_shape=jax.ShapeDtypeStruct((b, t_all, MLA_W), BF16),
        grid=(b, t_all // tq),
        in_specs=[
            pl.BlockSpec((None, tq, MLA_HEADS * HEAD_PAD), lambda bi, i: (bi, i, 0)),
            pl.BlockSpec((None, t_all, MLA_HEADS * HEAD_PAD), lambda bi, i: (bi, 0, 0)),
            pl.BlockSpec((None, t_all, MLA_W), lambda bi, i: (bi, 0, 0)),
        ],
        out_specs=pl.BlockSpec((None, tq, MLA_W), lambda bi, i: (bi, i, 0)),
        compiler_params=_cparams(("parallel", "parallel")),
        name="mla_attention",
    )(qc, kc, v)


def _fourier_kernel(cm_ref, sm_ref, pq_ref, o_ref):
    y = _dot(cm_ref[...], pq_ref[:, 0:FNET_W]) + _dot(sm_ref[...], pq_ref[:, FNET_W:])
    o_ref[...] = y.astype(BF16)


def _fourier(pq, cm, sm, tm, row0):
    b = pq.shape[0]
    t = cm.shape[0]
    return pl.pallas_call(
        _fourier_kernel,
        out_shape=jax.ShapeDtypeStruct((b, t, FNET_W), BF16),
        grid=(t // tm, b),
        in_specs=[
            pl.BlockSpec((tm, t), lambda i, bi: (i, 0)),
            pl.BlockSpec((tm, t), lambda i, bi: (i, 0)),
            pl.BlockSpec((None, t, 2 * FNET_W), lambda i, bi: (bi, row0 // t, 0)),
        ],
        out_specs=pl.BlockSpec((None, tm, FNET_W), lambda i, bi: (bi, i, 0)),
        compiler_params=_cparams(("parallel", "parallel")),
        name="fourier_mix",
    )(cm, sm, pq)


def _gla_chunk(q, k, v, f, tri, ind_ref, dmask_ref, st_ref, reverse):
    c = GLA_CHUNK
    f0 = f.astype(BF16)
    r1 = f - f0.astype(F32)
    f1 = r1.astype(BF16)
    f2 = (r1 - f1.astype(F32)).astype(BF16)
    g = _dot(tri, f0) + _dot(tri, f1) + _dot(tri, f2)
    e = g[0:1, :] if reverse else g[c - 1:c, :]
    qf = q.astype(F32)
    kf = k.astype(F32)
    qg = (qf * jnp.exp(g)).astype(BF16)
    kd = (kf * jnp.exp(e - g)).astype(BF16)
    st = st_ref[...]
    st_b = st.astype(BF16)

    sb = GLA_SUB
    ind = ind_ref[...]
    row_sb = lax.broadcasted_iota(I32, (sb, GLA_QK_W), 0)
    row_c = lax.broadcasted_iota(I32, (c, GLA_QK_W), 0)
    a_rows = []
    for i in range(c // sb):
        lo = i * sb
        gb, kb, qb = g[lo:lo + sb], kf[lo:lo + sb], qf[lo:lo + sb]
        ws = []
        for t in range(sb):
            w = jnp.exp(jnp.minimum(gb[t:t + 1] - gb, 0.0)) * kb * qb[t:t + 1]
            keep = (row_sb >= t) if reverse else (row_sb <= t)
            ws.append(jnp.where(keep, w, 0.0))
        w_all = jnp.concatenate(ws, axis=0).astype(BF16)
        r = _dot(w_all, ind) * dmask_ref[i]
        a_i = jnp.sum(r.reshape(sb, sb, GLA_QK_W), axis=1)
        ref_row = lo + sb if reverse else lo - 1
        if 0 <= ref_row < c:
            gr = g[ref_row:ref_row + 1]
            qt = (qb * jnp.exp(gb - gr)).astype(BF16)
            valid = (row_c >= lo + sb) if reverse else (row_c < lo)
            kt = jnp.where(valid, kf * jnp.exp(jnp.minimum(gr - g, 0.0)), 0.0).astype(BF16)
            kbd = jnp.tile(kt, (GLA_HEADS, 1)) * ind
            a_i = a_i + _dot_nt(qt, kbd)
        a_rows.append(a_i)
    a = jnp.concatenate(a_rows, axis=0).astype(BF16)

    outs, upd = [], []
    for hd in range(GLA_HEADS):
        ks_ = slice(hd * GLA_DK, (hd + 1) * GLA_DK)
        vh = v[:, hd * GLA_DV:(hd + 1) * GLA_DV]
        o_inter = _dot_nt(qg[:, ks_], st_b[:, ks_])
        o_intra = _dot(a[:, ks_], vh)
        outs.append(o_inter + o_intra)
        upd.append(_dot_tn(vh, kd[:, ks_]))
    st_ref[...] = st * jnp.exp(e) + jnp.concatenate(upd, axis=1)
    return jnp.concatenate(outs, axis=1)


def _gla_kernel(qf_ref, kf_ref, vf_ref, ff_ref, qb_ref, kb_ref, vb_ref, fb_ref,
                tril_ref, triu_ref, ind_ref, dmask_ref, of_ref, ob_ref,
                stf, stb, *, n_chunks):
    @pl.when(pl.program_id(1) == 0)
    def _():
        stf[...] = jnp.zeros_like(stf)
        stb[...] = jnp.zeros_like(stb)

    c = GLA_CHUNK
    for j in range(n_chunks):
        rows = slice(j * c, (j + 1) * c)
        of_ref[rows, :] = _gla_chunk(qf_ref[rows, :], kf_ref[rows, :], vf_ref[rows, :], ff_ref[rows, :],
                                     tril_ref[...], ind_ref, dmask_ref, stf, False)
    for j in reversed(range(n_chunks)):
        rows = slice(j * c, (j + 1) * c)
        ob_ref[rows, :] = _gla_chunk(qb_ref[rows, :], kb_ref[rows, :], vb_ref[rows, :], fb_ref[rows, :],
                                     triu_ref[...], ind_ref, dmask_ref, stb, True)


def _gla(gq, gk, gv, lfb, tb, seq, ctx):
    b, t_all, _ = gq.shape
    nlb, ncb = seq // tb, ctx // tb
    c = GLA_CHUNK

    def fwd(i):
        return jnp.where(i < ncb, nlb + i, i - ncb)

    def bwd(i):
        return jnp.where(i < ncb, nlb + ncb - 1 - i, nlb - 1 - (i - ncb))

    def spec(w, order, col=0):
        return pl.BlockSpec((None, tb, w), lambda bi, i: (bi, order(i), col))

    r = np.arange(c)
    tril = jnp.asarray(r[:, None] >= r[None, :], BF16)
    triu = jnp.asarray(r[:, None] <= r[None, :], BF16)
    hs = np.arange(GLA_QK_W)
    ind = jnp.asarray(hs[:, None] // GLA_DK == hs[None, :] // GLA_DK, BF16)
    sb = GLA_SUB
    s_loc = np.tile(np.arange(sb), sb)
    dmask = jnp.asarray(np.stack([(i * sb + s_loc)[:, None] == (hs % GLA_DK)[None, :] for i in range(c // sb)]),
                        F32)

    def cspec(a):
        return pl.BlockSpec(a.shape, lambda bi, i: (0,) * a.ndim)

    return pl.pallas_call(
        functools.partial(_gla_kernel, n_chunks=tb // c),
        out_shape=[jax.ShapeDtypeStruct((b, t_all, GLA_W), F32)] * 2,
        grid=(b, nlb + ncb),
        in_specs=[spec(GLA_QK_W, fwd), spec(GLA_QK_W, fwd), spec(GLA_W, fwd), spec(GLA_QK_W, fwd, 0),
                  spec(GLA_QK_W, bwd), spec(GLA_QK_W, bwd), spec(GLA_W, bwd), spec(GLA_QK_W, bwd, 1),
                  cspec(tril), cspec(triu), cspec(ind), cspec(dmask)],
        out_specs=[spec(GLA_W, fwd), spec(GLA_W, bwd)],
        scratch_shapes=[pltpu.VMEM((GLA_DV, GLA_QK_W), F32), pltpu.VMEM((GLA_DV, GLA_QK_W), F32)],
        compiler_params=_cparams(("parallel", "arbitrary")),
        name="gla_scan",
    )(gq, gk, gv, lfb, gq, gk, gv, lfb, tril, triu, ind, dmask)


def _merge_kernel(x_ref, mod_ref, om_ref, yfl_ref, yfc_ref, of_ref, ob_ref, og_ref, gate_ref,
                  gng_ref, wbm_ref, wbf_ref, wbg_ref, wo_ref, g2_ref, rw_ref, rb_ref,
                  x_out, h_out, idx_out, gw_out, *, n_lat_tiles):
    d = D_MODEL
    m = mod_ref[...]
    y_fn = jnp.where(pl.program_id(1) < n_lat_tiles, yfl_ref[...], yfc_ref[...])
    o = of_ref[...] + ob_ref[...]
    gng = gng_ref[...]
    og = og_ref[...].astype(F32)
    parts = []
    for hd in range(GLA_HEADS):
        sl = slice(hd * GLA_DV, (hd + 1) * GLA_DV)
        parts.append(_rms(o[:, sl], gng[:, sl]))
    o_gla = (jnp.concatenate(parts, axis=1) * (og * jax.nn.sigmoid(og))).astype(BF16)

    gate = jax.nn.sigmoid(gate_ref[...].astype(F32))
    y = (gate[:, 0:d] * _dot(om_ref[...], wbm_ref[...])
         + gate[:, d:2 * d] * _dot(y_fn, wbf_ref[...])
         + gate[:, 2 * d:3 * d] * _dot(o_gla, wbg_ref[...]))
    x1 = x_ref[...] + m[:, 2 * d:3 * d] * _dot(y.astype(BF16), wo_ref[...])
    x_out[...] = x1
    h2 = _rms(x1, g2_ref[...]) * (1.0 + m[:, 4 * d:5 * d]) + m[:, 3 * d:4 * d]
    h_out[...] = _pack_bf16_pairs(h2)

    logits = _dot(h2.astype(BF16), rw_ref[...]) + rb_ref[...]
    lane = lax.broadcasted_iota(I32, logits.shape, 1)
    lane_f = lane.astype(F32)
    neg = jnp.float32(-jnp.inf)
    lg = jnp.where(lane < N_EXPERTS, logits, neg)
    idx_acc = jnp.zeros(logits.shape, I32)
    val_acc = jnp.zeros(logits.shape, F32)
    v0 = None
    for kk in range(TOP_K):
        mx = jnp.max(lg, axis=-1, keepdims=True)
        ix = jnp.min(jnp.where(lg == mx, lane_f, float(ROUTER_PAD)), axis=-1, keepdims=True).astype(I32)
        if kk == 0:
            v0 = mx
        idx_acc = jnp.where(lane == kk, ix, idx_acc)
        val_acc = jnp.where(lane == kk, jnp.exp(mx - v0), val_acc)
        lg = jnp.where(lane == ix, neg, lg)
    idx_out[...] = idx_acc
    gw_out[...] = val_acc / jnp.sum(val_acc, axis=-1, keepdims=True)


def _merge(xa, mod, o_mla, y_fn_lat, y_fn_ctx, o_f, o_b, og, gate, lw, tm, n_lat_tiles):
    b, t_all, d = xa.shape

    def tspec(w):
        return pl.BlockSpec((None, tm, w), lambda bi, i: (bi, i, 0))

    yfl_spec = pl.BlockSpec((None, tm, FNET_W), lambda bi, i: (bi, jnp.minimum(i, n_lat_tiles - 1), 0))
    yfc_spec = pl.BlockSpec((None, tm, FNET_W), lambda bi, i: (bi, jnp.maximum(i - n_lat_tiles, 0), 0))

    def cspec(a):
        return pl.BlockSpec(a.shape, lambda bi, i: (0,) * a.ndim)

    consts = [lw["gla_norm_g"], lw["w_br_mla"], lw["w_br_fnet"], lw["w_br_gla"], lw["w_o"],
              lw["norm2_g"], lw["router_w"], lw["router_b"]]
    return pl.pallas_call(
        functools.partial(_merge_kernel, n_lat_tiles=n_lat_tiles),
        out_shape=[jax.ShapeDtypeStruct((b, t_all, d), F32), jax.ShapeDtypeStruct((b, t_all, d // 2), U32),
                   jax.ShapeDtypeStruct((b, t_all, ROUTER_PAD), I32),
                   jax.ShapeDtypeStruct((b, t_all, ROUTER_PAD), F32)],
        grid=(b, t_all // tm),
        in_specs=[
            tspec(d),
            pl.BlockSpec((None, None, 1, 6 * d), lambda bi, i: (bi, jnp.where(i < n_lat_tiles, 0, 1), 0, 0)),
            tspec(MLA_W), yfl_spec, yfc_spec, tspec(GLA_W), tspec(GLA_W), tspec(GLA_W), tspec(3 * d),
        ] + [cspec(a) for a in consts],
        out_specs=[tspec(d), tspec(d // 2), tspec(ROUTER_PAD), tspec(ROUTER_PAD)],
        input_output_aliases={0: 0},
        compiler_params=_cparams(("parallel", "parallel")),
        name="merge_router",
    )(xa, mod, o_mla, y_fn_lat, y_fn_ctx, o_f, o_b, og, gate, *consts)


def _row_gather(idx_smem, slot, src_hbm, dst, sem, n_rows):
    def body(r, carry):
        pltpu.make_async_copy(src_hbm.at[pl.ds(idx_smem[slot, r], 1)], dst.at[slot, pl.ds(r, 1)],
                              sem.at[slot]).start()
        return carry

    lax.fori_loop(0, n_rows, body, 0)


def _moe_kernel(blk_e_ref, tok_hbm, h_hbm, wu_ref, bu_ref, wd_ref, bd_ref, y_out,
                idx_smem, xbuf, isem, gsem):
    i = pl.program_id(0)
    n = pl.num_programs(0)
    slot = lax.rem(i, 2)
    nxt = 1 - slot

    def idx_copy(blk, s):
        return pltpu.make_async_copy(tok_hbm.at[blk], idx_smem.at[s], isem.at[s])

    @pl.when(i == 0)
    def _():
        idx_copy(0, 0).start()
        idx_copy(0, 0).wait()
        _row_gather(idx_smem, 0, h_hbm, xbuf, gsem, MOE_BLOCK)

        @pl.when(n > 1)
        def _():
            idx_copy(1, 1).start()

    @pl.when(i + 1 < n)
    def _():
        idx_copy(i + 1, nxt).wait()
        _row_gather(idx_smem, nxt, h_hbm, xbuf, gsem, MOE_BLOCK)

    @pl.when(i + 2 < n)
    def _():
        idx_copy(i + 2, slot).start()

    pltpu.make_async_copy(xbuf.at[slot], xbuf.at[slot], gsem.at[slot]).wait()
    xb = jnp.concatenate(_unpack_bf16_pairs(xbuf[slot]), axis=1).astype(BF16)
    up = _dot(xb, wu_ref[...]) + bu_ref[...]
    glu = jnp.minimum(up[:, 0:D_EXPERT], SWIGLU_LIMIT)
    lin = jnp.clip(up[:, D_EXPERT:], -SWIGLU_LIMIT, SWIGLU_LIMIT)
    act = glu * jax.nn.sigmoid(SWIGLU_ALPHA * glu) * (lin + 1.0)
    yb = _dot(act.astype(BF16), wd_ref[...]) + bd_ref[...]
    y_out[...] = _pack_bf16_pairs(yb)


def _moe_experts(blk_e, row_tok, h_flat, lw):
    n_blocks = blk_e.shape[0]
    d = D_MODEL
    grid_spec = pltpu.PrefetchScalarGridSpec(
        num_scalar_prefetch=1,
        grid=(n_blocks,),
        in_specs=[
            pl.BlockSpec(memory_space=pl.ANY),
            pl.BlockSpec(memory_space=pl.ANY),
            pl.BlockSpec((None, d, 2 * D_EXPERT), lambda i, be: (be[i], 0, 0)),
            pl.BlockSpec((None, 1, 2 * D_EXPERT), lambda i, be: (be[i], 0, 0)),
            pl.BlockSpec((None, D_EXPERT, d), lambda i, be: (be[i], 0, 0)),
            pl.BlockSpec((None, 1, d), lambda i, be: (be[i], 0, 0)),
        ],
        out_specs=pl.BlockSpec((MOE_BLOCK, d // 2), lambda i, be: (i, 0)),
        scratch_shapes=[pltpu.SMEM((2, MOE_BLOCK), I32), pltpu.VMEM((2, MOE_BLOCK, d // 2), U32),
                        pltpu.SemaphoreType.DMA((2,)), pltpu.SemaphoreType.DMA((2,))],
    )
    return pl.pallas_call(
        _moe_kernel,
        out_shape=jax.ShapeDtypeStruct((n_blocks * MOE_BLOCK, d // 2), U32),
        grid_spec=grid_spec,
        compiler_params=_cparams(("arbitrary",)),
        name="moe_experts",
    )(blk_e, row_tok, h_flat, lw["w_up"], lw["b_up"], lw["w_down"], lw["b_down"])


def _combine_kernel(pos_hbm, y_hbm, x_ref, mod_ref, gw_ref, x_out, idx_smem, ybuf, isem, gsem, *, tm, n_tiles):
    bi = pl.program_id(0)
    ti = pl.program_id(1)
    i = bi * n_tiles + ti
    n = pl.num_programs(0) * n_tiles
    slot = lax.rem(i, 2)
    nxt = 1 - slot
    n_rows = tm * TOP_K

    def idx_copy(blk, s):
        return pltpu.make_async_copy(pos_hbm.at[blk], idx_smem.at[s], isem.at[s])

    @pl.when(i == 0)
    def _():
        idx_copy(0, 0).start()
        idx_copy(0, 0).wait()
        _row_gather(idx_smem, 0, y_hbm, ybuf, gsem, n_rows)

        @pl.when(n > 1)
        def _():
            idx_copy(1, 1).start()

    @pl.when(i + 1 < n)
    def _():
        idx_copy(i + 1, nxt).wait()
        _row_gather(idx_smem, nxt, y_hbm, ybuf, gsem, n_rows)

    @pl.when(i + 2 < n)
    def _():
        idx_copy(i + 2, slot).start()

    pltpu.make_async_copy(ybuf.at[slot], ybuf.at[slot], gsem.at[slot]).wait()
    gw = gw_ref[...]
    acc_lo = acc_hi = None
    for kk in range(TOP_K):
        lo, hi = _unpack_bf16_pairs(ybuf[slot, kk * tm:(kk + 1) * tm, :])
        gk = gw[:, kk:kk + 1]
        acc_lo = lo * gk if kk == 0 else acc_lo + lo * gk
        acc_hi = hi * gk if kk == 0 else acc_hi + hi * gk
    d = D_MODEL
    x_out[...] = x_ref[...] + mod_ref[:, 5 * d:6 * d] * jnp.concatenate([acc_lo, acc_hi], axis=1)


def _combine(xa, mod, gates, pos, y_sorted, tm, n_lat_tiles):
    b, t_all, d = xa.shape
    n_tiles = t_all // tm
    return pl.pallas_call(
        functools.partial(_combine_kernel, tm=tm, n_tiles=n_tiles),
        out_shape=jax.ShapeDtypeStruct((b, t_all, d), F32),
        grid=(b, n_tiles),
        in_specs=[
            pl.BlockSpec(memory_space=pl.ANY),
            pl.BlockSpec(memory_space=pl.ANY),
            pl.BlockSpec((None, tm, d), lambda bi, i: (bi, i, 0)),
            pl.BlockSpec((None, None, 1, 6 * d), lambda bi, i: (bi, jnp.where(i < n_lat_tiles, 0, 1), 0, 0)),
            pl.BlockSpec((None, tm, ROUTER_PAD), lambda bi, i: (bi, i, 0)),
        ],
        out_specs=pl.BlockSpec((None, tm, d), lambda bi, i: (bi, i, 0)),
        scratch_shapes=[pltpu.SMEM((2, tm * TOP_K), I32), pltpu.VMEM((2, tm * TOP_K, d // 2), U32),
                        pltpu.SemaphoreType.DMA((2,)), pltpu.SemaphoreType.DMA((2,))],
        input_output_aliases={2: 0},
        compiler_params=_cparams(("arbitrary", "arbitrary")),
        name="moe_combine",
    )(pos, y_sorted, xa, mod, gates)


def _final_kernel(x_ref, g_ref, o_ref):
    o_ref[...] = _rms(x_ref[...], g_ref[...])


def _final_norm(xa, g, seq, tm):
    b, _, d = xa.shape
    return pl.pallas_call(
        _final_kernel,
        out_shape=jax.ShapeDtypeStruct((b, seq, d), F32),
        grid=(b, seq // tm),
        in_specs=[pl.BlockSpec((None, tm, d), lambda bi, i: (bi, i, 0)),
                  pl.BlockSpec((1, d), lambda bi, i: (0, 0))],
        out_specs=pl.BlockSpec((None, tm, d), lambda bi, i: (bi, i, 0)),
        compiler_params=_cparams(("parallel", "parallel")),
        name="final_norm",
    )(xa, g.reshape(1, d))


def _pad_cols(w, width):
    return jnp.pad(w, ((0, 0), (0, width - w.shape[1])))


def _rope_partner(w):
    a = ROPE_AXIS // 2
    return jnp.concatenate([-w[:, a:2 * a], w[:, 0:a], -w[:, 3 * a:4 * a], w[:, 2 * a:3 * a]], axis=1)


def _layer_weights(l, w_in, mla_q_norm_g, mla_w_uq, mla_kv_norm_g, mla_w_ukv, gla_w_gate_f, gla_b_gate_f,
                   gla_w_gate_b, gla_b_gate_b, gla_norm_g, w_br_mla, w_br_fnet, w_br_gla, w_o, norm1_g,
                   norm2_g, router_w, router_b, exp_w_up, exp_b_up, exp_w_down, exp_b_down):
    d = D_MODEL
    splits = np.cumsum([MLA_Q_RANK, MLA_KV_RANK + MLA_ROPE, FNET_W, GLA_QK_W, GLA_QK_W, GLA_W, GLA_W,
                        GLA_GATE_RANK, GLA_GATE_RANK])
    wq, wkv, wfn, wgq, wgk, wgv, wog, wgf, wgb, wgate = jnp.split(w_in[l], [int(s) for s in splits], axis=1)
    w_in_p = jnp.concatenate([wq, _pad_cols(wkv, 256), wfn, wgq, wgk, wgv, wog,
                              _pad_cols(jnp.concatenate([wgf, wgb], axis=1), 128), wgate], axis=1).astype(BF16)

    qk = MLA_NOPE + MLA_ROPE
    zq = jnp.zeros((MLA_Q_RANK, HEAD_PAD - qk), F32)
    zn = jnp.zeros((MLA_Q_RANK, MLA_NOPE), F32)
    wqa, wqb = [], []
    for hd in range(MLA_HEADS):
        wh = mla_w_uq[l][:, hd * qk:(hd + 1) * qk]
        wqa.append(jnp.concatenate([wh, zq], axis=1))
        wqb.append(jnp.concatenate([zn, _rope_partner(wh[:, MLA_NOPE:]), zq], axis=1))
    eye = jnp.eye(MLA_ROPE, dtype=F32)
    wka, wkb, wv = [], [], []
    for hd in range(MLA_HEADS):
        wh = mla_w_ukv[l][:, hd * (MLA_NOPE + MLA_V):(hd + 1) * (MLA_NOPE + MLA_V)]
        top = jnp.concatenate([wh[:, 0:MLA_NOPE], jnp.zeros((MLA_KV_RANK, HEAD_PAD - MLA_NOPE), F32)], axis=1)
        mid_a = jnp.concatenate([jnp.zeros((MLA_ROPE, MLA_NOPE), F32), eye,
                                 jnp.zeros((MLA_ROPE, HEAD_PAD - qk), F32)], axis=1)
        mid_b = jnp.concatenate([jnp.zeros((MLA_ROPE, MLA_NOPE), F32), _rope_partner(eye),
                                 jnp.zeros((MLA_ROPE, HEAD_PAD - qk), F32)], axis=1)
        bot = jnp.zeros((256 - MLA_KV_RANK - MLA_ROPE, HEAD_PAD), F32)
        wka.append(jnp.concatenate([top, mid_a, bot], axis=0))
        wkb.append(jnp.concatenate([jnp.zeros_like(top), mid_b, bot], axis=0))
        wv.append(jnp.concatenate([wh[:, MLA_NOPE:], jnp.zeros((256 - MLA_KV_RANK, MLA_V), F32)], axis=0))

    ch = np.arange(FNET_GROUP_W)
    ang = 2.0 * np.pi * ((ch[:, None] * ch[None, :]) % FNET_GROUP_W) / FNET_GROUP_W
    wc = np.concatenate([np.cos(ang), np.sin(ang)], axis=1) / math.sqrt(FNET_GROUP_W)

    wg = jnp.zeros((128, 2 * GLA_QK_W), F32)
    wg = wg.at[0:GLA_GATE_RANK, 0:GLA_QK_W].set(gla_w_gate_f[l])
    wg = wg.at[GLA_GATE_RANK:2 * GLA_GATE_RANK, GLA_QK_W:].set(gla_w_gate_b[l])

    return {
        "norm1_g": norm1_g[l].reshape(1, d),
        "w_in": w_in_p,
        "q_norm_g": mla_q_norm_g[l].reshape(1, -1),
        "wqa": jnp.concatenate(wqa, axis=1).astype(BF16),
        "wqb": jnp.concatenate(wqb, axis=1).astype(BF16),
        "kv_norm_g": mla_kv_norm_g[l].reshape(1, -1),
        "wka": jnp.concatenate(wka, axis=1).astype(BF16),
        "wkb": jnp.concatenate(wkb, axis=1).astype(BF16),
        "wv": jnp.concatenate(wv, axis=1).astype(BF16),
        "wc": jnp.asarray(wc, BF16),
        "wg": wg.astype(BF16),
        "bg": jnp.concatenate([gla_b_gate_f[l], gla_b_gate_b[l]]).reshape(1, -1),
        "gla_norm_g": gla_norm_g[l].reshape(1, -1),
        "w_br_mla": w_br_mla[l].astype(BF16),
        "w_br_fnet": w_br_fnet[l].astype(BF16),
        "w_br_gla": w_br_gla[l].astype(BF16),
        "w_o": w_o[l].astype(BF16),
        "norm2_g": norm2_g[l].reshape(1, d),
        "router_w": _pad_cols(router_w[l], ROUTER_PAD).astype(BF16),
        "router_b": _pad_cols(router_b[l].reshape(1, -1), ROUTER_PAD),
        "w_up": exp_w_up[l].astype(BF16),
        "b_up": exp_b_up[l].reshape(N_EXPERTS, 1, -1),
        "w_down": exp_w_down[l].astype(BF16),
        "b_down": exp_b_down[l].reshape(N_EXPERTS, 1, -1),
    }


def _rope_tables(seq, ctx):
    rows = seq // GRID_W
    row = jnp.repeat(jnp.arange(rows, dtype=F32), GRID_W)
    col = jnp.tile(jnp.arange(GRID_W, dtype=F32), rows)
    inv_freq = ROPE_BASE ** (-jnp.arange(0, ROPE_AXIS, 2, dtype=F32) / ROPE_AXIS)
    ang_r = row[:, None] * inv_freq
    ang_c = col[:, None] * inv_freq
    ang = jnp.concatenate([ang_r, ang_r, ang_c, ang_c], axis=1)
    ang = jnp.concatenate([ang, jnp.zeros((ctx, MLA_ROPE), F32)], axis=0)
    t_all = seq + ctx
    pad = jnp.zeros((t_all, HEAD_PAD - MLA_NOPE - MLA_ROPE), F32)
    cos_t = jnp.concatenate([jnp.ones((t_all, MLA_NOPE), F32), jnp.cos(ang), pad], axis=1)
    sin_t = jnp.concatenate([jnp.zeros((t_all, MLA_NOPE), F32), jnp.sin(ang), pad], axis=1)
    return cos_t, sin_t


def _dft_mats(t):
    r = np.arange(t, dtype=np.int64)
    ang = 2.0 * np.pi * ((r[:, None] * r[None, :]) % t) / t
    s = 1.0 / math.sqrt(t)
    return jnp.asarray(np.cos(ang) * s, BF16), jnp.asarray(-np.sin(ang) * s, BF16)


def _routing(top_idx, n_tok):
    n_assign = n_tok * TOP_K
    flat_e = top_idx.reshape(n_assign)
    onehot = (flat_e[:, None] == jnp.arange(N_EXPERTS, dtype=I32)[None, :]).astype(I32)
    csum = jnp.cumsum(onehot, axis=0)
    counts = csum[-1]
    rank = jnp.sum(csum * onehot, axis=1) - 1
    padded = ((counts + MOE_BLOCK - 1) // MOE_BLOCK) * MOE_BLOCK
    pad_end = jnp.cumsum(padded)
    pad_start = pad_end - padded
    dest = (pad_start[flat_e] + rank).astype(I32)
    n_blocks = -(-n_assign // MOE_BLOCK) + N_EXPERTS
    n_rows = n_blocks * MOE_BLOCK
    row_tok = jnp.zeros((n_rows,), I32).at[dest].set(jnp.arange(n_assign, dtype=I32) // TOP_K)
    blk_e = jnp.minimum(jnp.searchsorted(pad_end, jnp.arange(n_blocks, dtype=I32) * MOE_BLOCK, side="right"),
                        N_EXPERTS - 1).astype(I32)
    return blk_e, row_tok.reshape(n_blocks, MOE_BLOCK), dest


def kernel(x, c, ctx, c_ctx, w_mod, b_mod, norm1_g, w_in, mla_q_norm_g, mla_w_uq, mla_kv_norm_g, mla_w_ukv,
           gla_w_gate_f, gla_b_gate_f, gla_w_gate_b, gla_b_gate_b, gla_norm_g, w_br_mla, w_br_fnet, w_br_gla,
           w_o, norm2_g, router_w, router_b, exp_w_up, exp_b_up, exp_w_down, exp_b_down, final_norm_g):
    b, seq, d = x.shape
    n_ctx = ctx.shape[1]
    n_layers = w_mod.shape[0]
    t_all = seq + n_ctx
    assert d == D_MODEL and seq % n_ctx == 0 and seq % GRID_W == 0 and n_ctx % GLA_CHUNK == 0
    tm = min(256, n_ctx)
    assert n_ctx % tm == 0 and seq % tm == 0 and (tm * TOP_K) % MOE_BLOCK == 0
    n_lat_tiles = seq // tm
    t_fourier = min(512, seq)

    xa = jnp.concatenate([x, ctx], axis=1)

    rows = -(-(b + 1) // 8) * 8
    cv = jnp.zeros((rows, d), F32).at[0:b].set(c).at[b].set(c_ctx)
    mod_all = _mod_all(cv, w_mod, b_mod)
    cos_t, sin_t = _rope_tables(seq, n_ctx)
    cm_l, sm_l = _dft_mats(seq)
    cm_c, sm_c = _dft_mats(n_ctx)

    n_tok = b * t_all
    for l in range(n_layers):
        lw = _layer_weights(l, w_in, mla_q_norm_g, mla_w_uq, mla_kv_norm_g, mla_w_ukv, gla_w_gate_f,
                            gla_b_gate_f, gla_w_gate_b, gla_b_gate_b, gla_norm_g, w_br_mla, w_br_fnet,
                            w_br_gla, w_o, norm1_g, norm2_g, router_w, router_b, exp_w_up, exp_b_up,
                            exp_w_down, exp_b_down)
        mod_lat = mod_all[l, 0:b]
        mod_ctx = jnp.broadcast_to(mod_all[l, b][None, :], (b, 6 * d))
        mod = jnp.stack([mod_lat, mod_ctx], axis=1).reshape(b, 2, 1, 6 * d)

        qc, kc, v, pq, gq, gk, gv, lfb, og, gate = _inproj(xa, mod, cos_t, sin_t, lw, tm, n_lat_tiles)
        o_mla = _attention(qc, kc, v, tm, seq, n_ctx)
        y_fn_lat = _fourier(pq, cm_l, sm_l, t_fourier, 0)
        y_fn_ctx = _fourier(pq, cm_c, sm_c, n_ctx, seq)
        o_f, o_b = _gla(gq, gk, gv, lfb, tm, seq, n_ctx)
        xa, h2, top_idx, gates = _merge(xa, mod, o_mla, y_fn_lat, y_fn_ctx, o_f, o_b, og, gate, lw, tm,
                                        n_lat_tiles)

        blk_e, row_tok, dest = _routing(top_idx[..., 0:TOP_K], n_tok)
        y_sorted = _moe_experts(blk_e, row_tok, h2.reshape(n_tok, d // 2), lw)
        pos = dest.reshape(n_tok // tm, tm, TOP_K).transpose(0, 2, 1).reshape(n_tok // tm, tm * TOP_K)
        xa = _combine(xa, mod, gates, pos, y_sorted, tm, n_lat_tiles)

    return _final_norm(xa, final_norm_g, seq, tm)
```

```python
import functools
import math

import jax
import jax.numpy as jnp
import numpy as np
from jax import lax
from jax.experimental import pallas as pl
from jax.experimental.pallas import tpu as pltpu

F32 = jnp.float32
BF16 = jnp.bfloat16
I32 = jnp.int32
U32 = jnp.uint32

D_MODEL = 1024
GRID_W = 64
EPS = 1e-6

MLA_HEADS = 8
MLA_Q_RANK = 256
MLA_KV_RANK = 128
MLA_NOPE = 64
MLA_ROPE = 32
MLA_V = 64
MLA_SCALE = (MLA_NOPE + MLA_ROPE) ** -0.5
ROPE_AXIS = MLA_ROPE // 2
ROPE_BASE = 10000.0
HEAD_PAD = 128

FNET_GROUPS = 4
FNET_GROUP_W = 128
FNET_W = FNET_GROUPS * FNET_GROUP_W

GLA_HEADS = 4
GLA_DK = 64
GLA_DV = 128
GLA_GATE_RANK = 16
GLA_TAU = 16.0
GLA_CHUNK = 64
GLA_SUB = 16
GLA_SCALE = GLA_DK ** -0.5
GLA_QK_W = GLA_HEADS * GLA_DK
GLA_W = GLA_HEADS * GLA_DV

N_EXPERTS = 32
TOP_K = 4
D_EXPERT = D_MODEL
SWIGLU_ALPHA = 1.702
SWIGLU_LIMIT = 7.0
MOE_BLOCK = 256
ROUTER_PAD = 128

MLA_W = MLA_HEADS * MLA_V

VMEM_LIMIT_V7X = 56 * 1024 * 1024

C_Q = 0
C_KV = C_Q + MLA_Q_RANK
C_FN = C_KV + 256
C_GQ = C_FN + FNET_W
C_GK = C_GQ + GLA_QK_W
C_GV = C_GK + GLA_QK_W
C_OG = C_GV + GLA_W
C_GG = C_OG + GLA_W
C_GATE = C_GG + 128
IN_WIDTH_PAD = C_GATE + 3 * D_MODEL


def _cparams(sem, vmem=VMEM_LIMIT_V7X):
    return pltpu.CompilerParams(dimension_semantics=sem, vmem_limit_bytes=vmem)


def _rms(x, g):
    return x * lax.rsqrt(jnp.mean(x * x, axis=-1, keepdims=True) + EPS) * g


def _dot(a, b):
    return jnp.dot(a, b, preferred_element_type=F32)


def _dot_nt(a, b):
    return lax.dot_general(a, b, (((1,), (1,)), ((), ())), preferred_element_type=F32)


def _dot_tn(a, b):
    return lax.dot_general(a, b, (((0,), (0,)), ((), ())), preferred_element_type=F32)


def _pack_bf16_pairs(x):
    w = x.shape[1] // 2
    r = lax.bitcast_convert_type(x.astype(BF16).astype(F32), U32)
    return r[:, w:] | (r[:, :w] >> 16)


def _unpack_bf16_pairs(u):
    lo = lax.bitcast_convert_type(u << 16, F32)
    hi = lax.bitcast_convert_type(u & jnp.uint32(0xFFFF0000), F32)
    return lo, hi


ROW_TILES = D_MODEL // 2 // 128


def _store_row_tiles(ref, x):
    for j in range(ROW_TILES):
        ref[:, j, :] = x[:, j * 128:(j + 1) * 128]


def _load_row_tiles(ref):
    return jnp.concatenate([ref[:, j, :] for j in range(ROW_TILES)], axis=1)


def _mod_kernel(cv_ref, w_ref, b_ref, o_ref):
    cv = cv_ref[...]
    s = cv * jax.nn.sigmoid(cv)
    o_ref[...] = jnp.dot(s, w_ref[...], preferred_element_type=F32,
                         precision=lax.Precision.HIGHEST) + b_ref[...]


def _mod_all(cv, w_mod, b_mod):
    n_layers, d, w6 = w_mod.shape
    rows = cv.shape[0]
    tn = 1536
    return pl.pallas_call(
        _mod_kernel,
        out_shape=jax.ShapeDtypeStruct((n_layers, rows, w6), F32),
        grid=(n_layers, w6 // tn),
        in_specs=[
            pl.BlockSpec((rows, d), lambda l, j: (0, 0)),
            pl.BlockSpec((None, d, tn), lambda l, j: (l, 0, j)),
            pl.BlockSpec((None, 1, tn), lambda l, j: (l, 0, j)),
        ],
        out_specs=pl.BlockSpec((None, rows, tn), lambda l, j: (l, 0, j)),
        compiler_params=_cparams(("parallel", "parallel")),
        name="adaln_mod",
    )(cv, w_mod, b_mod.reshape(n_layers, 1, w6))


def _inproj_kernel(x_ref, mod_ref, cos_ref, sin_ref, g1_ref, win_ref, qng_ref, wqa_ref, wqb_ref,
                   kvg_ref, wka_ref, wkb_ref, wv_ref, wc_ref, wg_ref, bg_ref,
                   q_out, k_out, v_out, pq_out, gq_out, gk_out, gv_out, lfb_out, og_out, gate_out):
    d = D_MODEL
    x = x_ref[...]
    m = mod_ref[...]
    h = (_rms(x, g1_ref[...]) * (1.0 + m[:, d:2 * d]) + m[:, 0:d]).astype(BF16)

    def proj(a, b):
        return _dot(h, win_ref[:, a:b])

    cos = jnp.tile(cos_ref[...], (1, MLA_HEADS))
    sin = jnp.tile(sin_ref[...], (1, MLA_HEADS))

    nq = _rms(proj(C_Q, C_KV), qng_ref[...]).astype(BF16)
    q = (_dot(nq, wqa_ref[...]) * cos + _dot(nq, wqb_ref[...]) * sin) * MLA_SCALE
    q_out[...] = q.astype(BF16)
    ukv = proj(C_KV, C_FN)
    ckn = _rms(ukv[:, 0:MLA_KV_RANK], kvg_ref[...])
    lhs = jnp.concatenate([ckn, ukv[:, MLA_KV_RANK:]], axis=1).astype(BF16)
    k_out[...] = (_dot(lhs, wka_ref[...]) * cos + _dot(lhs, wkb_ref[...]) * sin).astype(BF16)
    v_out[...] = _dot(lhs, wv_ref[...]).astype(BF16)

    ufn = proj(C_FN, C_GQ).astype(BF16)
    ps, qs = [], []
    for g in range(FNET_GROUPS):
        r = _dot(ufn[:, g * FNET_GROUP_W:(g + 1) * FNET_GROUP_W], wc_ref[...])
        ps.append(r[:, 0:FNET_GROUP_W])
        qs.append(r[:, FNET_GROUP_W:])
    pq_out[...] = jnp.concatenate(ps + qs, axis=1).astype(BF16)

    gq_out[...] = (proj(C_GQ, C_GK) * GLA_SCALE).astype(BF16)
    gk_out[...] = proj(C_GK, C_GV).astype(BF16)
    gv_out[...] = proj(C_GV, C_OG).astype(BF16)
    og_out[...] = proj(C_OG, C_GG).astype(BF16)
    z = _dot(proj(C_GG, C_GATE).astype(BF16), wg_ref[...]) + bg_ref[...]
    lfb_out[...] = (jnp.minimum(z, 0.0) - jnp.log1p(jnp.exp(-jnp.abs(z)))) * (1.0 / GLA_TAU)
    gate_out[...] = proj(C_GATE, IN_WIDTH_PAD).astype(BF16)


def _inproj(xa, mod, cos_t, sin_t, lw, tm, n_lat_tiles):
    b, t_all, d = xa.shape
    nt = t_all // tm

    def tok(w, dt):
        return jax.ShapeDtypeStruct((b, t_all, w), dt)

    def tspec(w):
        return pl.BlockSpec((None, tm, w), lambda bi, i: (bi, i, 0))

    def cspec(a):
        return pl.BlockSpec(a.shape, lambda bi, i: (0,) * a.ndim)

    consts = [lw["norm1_g"], lw["w_in"], lw["q_norm_g"], lw["wqa"], lw["wqb"], lw["kv_norm_g"],
              lw["wka"], lw["wkb"], lw["wv"], lw["wc"], lw["wg"], lw["bg"]]
    widths = [(1024, BF16), (1024, BF16), (MLA_W, BF16), (2 * FNET_W, BF16), (GLA_QK_W, BF16),
              (GLA_QK_W, BF16), (GLA_W, BF16), (2 * GLA_QK_W, F32), (GLA_W, BF16), (3 * D_MODEL, BF16)]
    return pl.pallas_call(
        _inproj_kernel,
        out_shape=[tok(w, dt) for w, dt in widths],
        grid=(b, nt),
        in_specs=[
            tspec(d),
            pl.BlockSpec((None, None, 1, 6 * d), lambda bi, i: (bi, jnp.where(i < n_lat_tiles, 0, 1), 0, 0)),
            pl.BlockSpec((tm, HEAD_PAD), lambda bi, i: (i, 0)),
            pl.BlockSpec((tm, HEAD_PAD), lambda bi, i: (i, 0)),
        ] + [cspec(a) for a in consts],
        out_specs=[tspec(w) for w, _ in widths],
        compiler_params=_cparams(("parallel", "parallel")),
        name="inproj",
    )(xa, mod, cos_t, sin_t, *consts)


def _attn_heads(q_ref, k_ref, v_ref, o_ref, k0, nk):
    lane = lax.broadcasted_iota(I32, (q_ref.shape[0], 2 * MLA_V), 1)
    outs = []
    for hp in range(MLA_HEADS // 2):
        vp = v_ref[k0:k0 + nk, hp * 2 * MLA_V:(hp + 1) * 2 * MLA_V]
        rs = []
        for j in range(2):
            hd = 2 * hp + j
            q = q_ref[:, hd * HEAD_PAD:(hd + 1) * HEAD_PAD]
            k = k_ref[k0:k0 + nk, hd * HEAD_PAD:(hd + 1) * HEAD_PAD]
            s = _dot_nt(q, k)
            p = jnp.exp(s - jnp.max(s, axis=-1, keepdims=True))
            l = jnp.sum(p, axis=-1, keepdims=True)
            rs.append(_dot(p.astype(BF16), vp) / l)
        outs.append(jnp.where(lane < MLA_V, rs[0], rs[1]))
    o_ref[...] = jnp.concatenate(outs, axis=1).astype(BF16)


def _attn_kernel(q_ref, k_ref, v_ref, o_ref, *, n_lat_tiles, seq, ctx):
    i = pl.program_id(1)

    @pl.when(i < n_lat_tiles)
    def _():
        _attn_heads(q_ref, k_ref, v_ref, o_ref, 0, seq + ctx)

    @pl.when(i >= n_lat_tiles)
    def _():
        _attn_heads(q_ref, k_ref, v_ref, o_ref, seq, ctx)


def _attention(qc, kc, v, tq, seq, ctx):
    b, t_all, _ = qc.shape
    return pl.pallas_call(
        functools.partial(_attn_kernel, n_lat_tiles=seq // tq, seq=seq, ctx=ctx),
        out_shape=jax.ShapeDtypeStruct((b, t_all, MLA_W), BF16),
        grid=(b, t_all // tq),
        in_specs=[
            pl.BlockSpec((None, tq, MLA_HEADS * HEAD_PAD), lambda bi, i: (bi, i, 0)),
            pl.BlockSpec((None, t_all, MLA_HEADS * HEAD_PAD), lambda bi, i: (bi, 0, 0)),
            pl.BlockSpec((None, t_all, MLA_W), lambda bi, i: (bi, 0, 0)),
        ],
        out_specs=pl.BlockSpec((None, tq, MLA_W), lambda bi, i: (bi, i, 0)),
        compiler_params=_cparams(("parallel", "parallel")),
        name="mla_attention",
    )(qc, kc, v)


def _fourier_kernel(cm_ref, sm_ref, pq_ref, o_ref):
    y = _dot(cm_ref[...], pq_ref[:, 0:FNET_W]) + _dot(sm_ref[...], pq_ref[:, FNET_W:])
    o_ref[...] = y.astype(BF16)


def _fourier(pq, cm, sm, tm, row0):
    b = pq.shape[0]
    t = cm.shape[0]
    return pl.pallas_call(
        _fourier_kernel,
        out_shape=jax.ShapeDtypeStruct((b, t, FNET_W), BF16),
        grid=(t // tm, b),
        in_specs=[
            pl.BlockSpec((tm, t), lambda i, bi: (i, 0)),
            pl.BlockSpec((tm, t), lambda i, bi: (i, 0)),
            pl.BlockSpec((None, t, 2 * FNET_W), lambda i, bi: (bi, row0 // t, 0)),
        ],
        out_specs=pl.BlockSpec((None, tm, FNET_W), lambda i, bi: (bi, i, 0)),
        compiler_params=_cparams(("parallel", "parallel")),
        name="fourier_mix",
    )(cm, sm, pq)


def _gla_chunk(q, k, v, f, tri, ind_ref, dmask_ref, st_ref, reverse):
    c = GLA_CHUNK
    f0 = f.astype(BF16)
    r1 = f - f0.astype(F32)
    f1 = r1.astype(BF16)
    f2 = (r1 - f1.astype(F32)).astype(BF16)
    g = _dot(tri, f0) + _dot(tri, f1) + _dot(tri, f2)
    e = g[0:1, :] if reverse else g[c - 1:c, :]
    qf = q.astype(F32)
    kf = k.astype(F32)
    qg = (qf * jnp.exp(g)).astype(BF16)
    kd = (kf * jnp.exp(e - g)).astype(BF16)
    st = st_ref[...]
    st_b = st.astype(BF16)

    sb = GLA_SUB
    ind = ind_ref[...]
    row_sb = lax.broadcasted_iota(I32, (sb, GLA_QK_W), 0)
    row_c = lax.broadcasted_iota(I32, (c, GLA_QK_W), 0)
    a_rows = []
    for i in range(c // sb):
        lo = i * sb
        gb, kb, qb = g[lo:lo + sb], kf[lo:lo + sb], qf[lo:lo + sb]
        ws = []
        for t in range(sb):
            w = jnp.exp(jnp.minimum(gb[t:t + 1] - gb, 0.0)) * kb * qb[t:t + 1]
            keep = (row_sb >= t) if reverse else (row_sb <= t)
            ws.append(jnp.where(keep, w, 0.0))
        w_all = jnp.concatenate(ws, axis=0).astype(BF16)
        r = _dot(w_all, ind) * dmask_ref[i]
        a_i = jnp.sum(r.reshape(sb, sb, GLA_QK_W), axis=1)
        ref_row = lo + sb if reverse else lo - 1
        if 0 <= ref_row < c:
            gr = g[ref_row:ref_row + 1]
            qt = (qb * jnp.exp(gb - gr)).astype(BF16)
            valid = (row_c >= lo + sb) if reverse else (row_c < lo)
            kt = jnp.where(valid, kf * jnp.exp(jnp.minimum(gr - g, 0.0)), 0.0).astype(BF16)
            kbd = jnp.tile(kt, (GLA_HEADS, 1)) * ind
            a_i = a_i + _dot_nt(qt, kbd)
        a_rows.append(a_i)
    a = jnp.concatenate(a_rows, axis=0).astype(BF16)

    outs, upd = [], []
    for hd in range(GLA_HEADS):
        ks_ = slice(hd * GLA_DK, (hd + 1) * GLA_DK)
        vh = v[:, hd * GLA_DV:(hd + 1) * GLA_DV]
        o_inter = _dot_nt(qg[:, ks_], st_b[:, ks_])
        o_intra = _dot(a[:, ks_], vh)
        outs.append(o_inter + o_intra)
        upd.append(_dot_tn(vh, kd[:, ks_]))
    st_ref[...] = st * jnp.exp(e) + jnp.concatenate(upd, axis=1)
    return jnp.concatenate(outs, axis=1)


def _gla_kernel(qf_ref, kf_ref, vf_ref, ff_ref, qb_ref, kb_ref, vb_ref, fb_ref,
                tril_ref, triu_ref, ind_ref, dmask_ref, of_ref, ob_ref,
                stf, stb, *, n_chunks):
    @pl.when(pl.program_id(1) == 0)
    def _():
        stf[...] = jnp.zeros_like(stf)
        stb[...] = jnp.zeros_like(stb)

    c = GLA_CHUNK
    for j in range(n_chunks):
        rows = slice(j * c, (j + 1) * c)
        of_ref[rows, :] = _gla_chunk(qf_ref[rows, :], kf_ref[rows, :], vf_ref[rows, :], ff_ref[rows, :],
                                     tril_ref[...], ind_ref, dmask_ref, stf, False)
    for j in reversed(range(n_chunks)):
        rows = slice(j * c, (j + 1) * c)
        ob_ref[rows, :] = _gla_chunk(qb_ref[rows, :], kb_ref[rows, :], vb_ref[rows, :], fb_ref[rows, :],
                                     triu_ref[...], ind_ref, dmask_ref, stb, True)


def _gla(gq, gk, gv, lfb, tb, seq, ctx):
    b, t_all, _ = gq.shape
    nlb, ncb = seq // tb, ctx // tb
    c = GLA_CHUNK

    def fwd(i):
        return jnp.where(i < ncb, nlb + i, i - ncb)

    def bwd(i):
        return jnp.where(i < ncb, nlb + ncb - 1 - i, nlb - 1 - (i - ncb))

    def spec(w, order, col=0):
        return pl.BlockSpec((None, tb, w), lambda bi, i: (bi, order(i), col))

    r = np.arange(c)
    tril = jnp.asarray(r[:, None] >= r[None, :], BF16)
    triu = jnp.asarray(r[:, None] <= r[None, :], BF16)
    hs = np.arange(GLA_QK_W)
    ind = jnp.asarray(hs[:, None] // GLA_DK == hs[None, :] // GLA_DK, BF16)
    sb = GLA_SUB
    s_loc = np.tile(np.arange(sb), sb)
    dmask = jnp.asarray(np.stack([(i * sb + s_loc)[:, None] == (hs % GLA_DK)[None, :] for i in range(c // sb)]),
                        F32)

    def cspec(a):
        return pl.BlockSpec(a.shape, lambda bi, i: (0,) * a.ndim)

    return pl.pallas_call(
        functools.partial(_gla_kernel, n_chunks=tb // c),
        out_shape=[jax.ShapeDtypeStruct((b, t_all, GLA_W), F32)] * 2,
        grid=(b, nlb + ncb),
        in_specs=[spec(GLA_QK_W, fwd), spec(GLA_QK_W, fwd), spec(GLA_W, fwd), spec(GLA_QK_W, fwd, 0),
                  spec(GLA_QK_W, bwd), spec(GLA_QK_W, bwd), spec(GLA_W, bwd), spec(GLA_QK_W, bwd, 1),
                  cspec(tril), cspec(triu), cspec(ind), cspec(dmask)],
        out_specs=[spec(GLA_W, fwd), spec(GLA_W, bwd)],
        scratch_shapes=[pltpu.VMEM((GLA_DV, GLA_QK_W), F32), pltpu.VMEM((GLA_DV, GLA_QK_W), F32)],
        compiler_params=_cparams(("parallel", "arbitrary")),
        name="gla_scan",
    )(gq, gk, gv, lfb, gq, gk, gv, lfb, tril, triu, ind, dmask)


def _merge_kernel(x_ref, mod_ref, om_ref, yfl_ref, yfc_ref, of_ref, ob_ref, og_ref, gate_ref,
                  gng_ref, wbm_ref, wbf_ref, wbg_ref, wo_ref, g2_ref, rw_ref, rb_ref,
                  x_out, h_out, idx_out, gw_out, *, n_lat_tiles):
    d = D_MODEL
    m = mod_ref[...]
    y_fn = jnp.where(pl.program_id(1) < n_lat_tiles, yfl_ref[...], yfc_ref[...])
    o = of_ref[...] + ob_ref[...]
    gng = gng_ref[...]
    og = og_ref[...].astype(F32)
    parts = []
    for hd in range(GLA_HEADS):
        sl = slice(hd * GLA_DV, (hd + 1) * GLA_DV)
        parts.append(_rms(o[:, sl], gng[:, sl]))
    o_gla = (jnp.concatenate(parts, axis=1) * (og * jax.nn.sigmoid(og))).astype(BF16)

    gate = jax.nn.sigmoid(gate_ref[...].astype(F32))
    y = (gate[:, 0:d] * _dot(om_ref[...], wbm_ref[...])
         + gate[:, d:2 * d] * _dot(y_fn, wbf_ref[...])
         + gate[:, 2 * d:3 * d] * _dot(o_gla, wbg_ref[...]))
    x1 = x_ref[...] + m[:, 2 * d:3 * d] * _dot(y.astype(BF16), wo_ref[...])
    x_out[...] = x1
    h2 = _rms(x1, g2_ref[...]) * (1.0 + m[:, 4 * d:5 * d]) + m[:, 3 * d:4 * d]
    _store_row_tiles(h_out, _pack_bf16_pairs(h2))

    logits = _dot(h2.astype(BF16), rw_ref[...]) + rb_ref[...]
    lane = lax.broadcasted_iota(I32, logits.shape, 1)
    lane_f = lane.astype(F32)
    neg = jnp.float32(-jnp.inf)
    lg = jnp.where(lane < N_EXPERTS, logits, neg)
    idx_acc = jnp.zeros(logits.shape, I32)
    val_acc = jnp.zeros(logits.shape, F32)
    v0 = None
    for kk in range(TOP_K):
        mx = jnp.max(lg, axis=-1, keepdims=True)
        ix = jnp.min(jnp.where(lg == mx, lane_f, float(ROUTER_PAD)), axis=-1, keepdims=True).astype(I32)
        if kk == 0:
            v0 = mx
        idx_acc = jnp.where(lane == kk, ix, idx_acc)
        val_acc = jnp.where(lane == kk, jnp.exp(mx - v0), val_acc)
        lg = jnp.where(lane == ix, neg, lg)
    idx_out[...] = idx_acc
    gw_out[...] = val_acc / jnp.sum(val_acc, axis=-1, keepdims=True)


def _merge(xa, mod, o_mla, y_fn_lat, y_fn_ctx, o_f, o_b, og, gate, lw, tm, n_lat_tiles):
    b, t_all, d = xa.shape

    def tspec(w):
        return pl.BlockSpec((None, tm, w), lambda bi, i: (bi, i, 0))

    yfl_spec = pl.BlockSpec((None, tm, FNET_W), lambda bi, i: (bi, jnp.minimum(i, n_lat_tiles - 1), 0))
    yfc_spec = pl.BlockSpec((None, tm, FNET_W), lambda bi, i: (bi, jnp.maximum(i - n_lat_tiles, 0), 0))

    def cspec(a):
        return pl.BlockSpec(a.shape, lambda bi, i: (0,) * a.ndim)

    consts = [lw["gla_norm_g"], lw["w_br_mla"], lw["w_br_fnet"], lw["w_br_gla"], lw["w_o"],
              lw["norm2_g"], lw["router_w"], lw["router_b"]]
    return pl.pallas_call(
        functools.partial(_merge_kernel, n_lat_tiles=n_lat_tiles),
        out_shape=[jax.ShapeDtypeStruct((b, t_all, d), F32), jax.ShapeDtypeStruct((b, t_all, ROW_TILES, 128), U32),
                   jax.ShapeDtypeStruct((b, t_all, ROUTER_PAD), I32),
                   jax.ShapeDtypeStruct((b, t_all, ROUTER_PAD), F32)],
        grid=(b, t_all // tm),
        in_specs=[
            tspec(d),
            pl.BlockSpec((None, None, 1, 6 * d), lambda bi, i: (bi, jnp.where(i < n_lat_tiles, 0, 1), 0, 0)),
            tspec(MLA_W), yfl_spec, yfc_spec, tspec(GLA_W), tspec(GLA_W), tspec(GLA_W), tspec(3 * d),
        ] + [cspec(a) for a in consts],
        out_specs=[tspec(d), pl.BlockSpec((None, tm, ROW_TILES, 128), lambda bi, i: (bi, i, 0, 0)),
                   tspec(ROUTER_PAD), tspec(ROUTER_PAD)],
        input_output_aliases={0: 0},
        compiler_params=_cparams(("parallel", "parallel")),
        name="merge_router",
    )(xa, mod, o_mla, y_fn_lat, y_fn_ctx, o_f, o_b, og, gate, *consts)


GATHER_UNROLL = 8


def _row_gather(idx_smem, slot, src_hbm, dst, sem, n_rows):
    def body(r, carry):
        pltpu.make_async_copy(src_hbm.at[pl.ds(idx_smem[slot, r], 1)], dst.at[slot, pl.ds(r, 1)],
                              sem.at[slot]).start()
        return carry

    lax.fori_loop(0, n_rows, body, 0, unroll=GATHER_UNROLL)


def _moe_kernel(blk_e_ref, tok_hbm, h_hbm, wu_ref, bu_ref, wd_ref, bd_ref, y_out,
                idx_smem, xbuf, isem, gsem):
    i = pl.program_id(0)
    n = pl.num_programs(0)
    slot = lax.rem(i, 2)
    nxt = 1 - slot

    def idx_copy(blk, s):
        return pltpu.make_async_copy(tok_hbm.at[blk], idx_smem.at[s], isem.at[s])

    @pl.when(i == 0)
    def _():
        idx_copy(0, 0).start()
        idx_copy(0, 0).wait()
        _row_gather(idx_smem, 0, h_hbm, xbuf, gsem, MOE_BLOCK)

        @pl.when(n > 1)
        def _():
            idx_copy(1, 1).start()

    @pl.when(i + 1 < n)
    def _():
        idx_copy(i + 1, nxt).wait()
        _row_gather(idx_smem, nxt, h_hbm, xbuf, gsem, MOE_BLOCK)

    @pl.when(i + 2 < n)
    def _():
        idx_copy(i + 2, slot).start()

    pltpu.make_async_copy(xbuf.at[slot], xbuf.at[slot], gsem.at[slot]).wait()
    xb = jnp.concatenate(_unpack_bf16_pairs(_load_row_tiles(xbuf.at[slot])), axis=1).astype(BF16)
    up = _dot(xb, wu_ref[...]) + bu_ref[...]
    glu = jnp.minimum(up[:, 0:D_EXPERT], SWIGLU_LIMIT)
    lin = jnp.clip(up[:, D_EXPERT:], -SWIGLU_LIMIT, SWIGLU_LIMIT)
    act = glu * jax.nn.sigmoid(SWIGLU_ALPHA * glu) * (lin + 1.0)
    yb = _dot(act.astype(BF16), wd_ref[...]) + bd_ref[...]
    _store_row_tiles(y_out, _pack_bf16_pairs(yb))


def _moe_experts(blk_e, row_tok, h_flat, lw):
    n_blocks = blk_e.shape[0]
    d = D_MODEL
    grid_spec = pltpu.PrefetchScalarGridSpec(
        num_scalar_prefetch=1,
        grid=(n_blocks,),
        in_specs=[
            pl.BlockSpec(memory_space=pl.ANY),
            pl.BlockSpec(memory_space=pl.ANY),
            pl.BlockSpec((None, d, 2 * D_EXPERT), lambda i, be: (be[i], 0, 0)),
            pl.BlockSpec((None, 1, 2 * D_EXPERT), lambda i, be: (be[i], 0, 0)),
            pl.BlockSpec((None, D_EXPERT, d), lambda i, be: (be[i], 0, 0)),
            pl.BlockSpec((None, 1, d), lambda i, be: (be[i], 0, 0)),
        ],
        out_specs=pl.BlockSpec((MOE_BLOCK, ROW_TILES, 128), lambda i, be: (i, 0, 0)),
        scratch_shapes=[pltpu.SMEM((2, MOE_BLOCK), I32), pltpu.VMEM((2, MOE_BLOCK, ROW_TILES, 128), U32),
                        pltpu.SemaphoreType.DMA((2,)), pltpu.SemaphoreType.DMA((2,))],
    )
    return pl.pallas_call(
        _moe_kernel,
        out_shape=jax.ShapeDtypeStruct((n_blocks * MOE_BLOCK, ROW_TILES, 128), U32),
        grid_spec=grid_spec,
        compiler_params=_cparams(("arbitrary",)),
        name="moe_experts",
    )(blk_e, row_tok, h_flat, lw["w_up"], lw["b_up"], lw["w_down"], lw["b_down"])


def _combine_kernel(pos_hbm, y_hbm, x_ref, mod_ref, gw_ref, x_out, idx_smem, ybuf, isem, gsem, *, tm, n_tiles):
    bi = pl.program_id(0)
    ti = pl.program_id(1)
    i = bi * n_tiles + ti
    n = pl.num_programs(0) * n_tiles
    slot = lax.rem(i, 2)
    nxt = 1 - slot
    n_rows = tm * TOP_K

    def idx_copy(blk, s):
        return pltpu.make_async_copy(pos_hbm.at[blk], idx_smem.at[s], isem.at[s])

    @pl.when(i == 0)
    def _():
        idx_copy(0, 0).start()
        idx_copy(0, 0).wait()
        _row_gather(idx_smem, 0, y_hbm, ybuf, gsem, n_rows)

        @pl.when(n > 1)
        def _():
            idx_copy(1, 1).start()

    @pl.when(i + 1 < n)
    def _():
        idx_copy(i + 1, nxt).wait()
        _row_gather(idx_smem, nxt, y_hbm, ybuf, gsem, n_rows)

    @pl.when(i + 2 < n)
    def _():
        idx_copy(i + 2, slot).start()

    pltpu.make_async_copy(ybuf.at[slot], ybuf.at[slot], gsem.at[slot]).wait()
    gw = gw_ref[...]
    acc_lo = acc_hi = None
    for kk in range(TOP_K):
        lo, hi = _unpack_bf16_pairs(_load_row_tiles(ybuf.at[slot, pl.ds(kk * tm, tm)]))
        gk = gw[:, kk:kk + 1]
        acc_lo = lo * gk if kk == 0 else acc_lo + lo * gk
        acc_hi = hi * gk if kk == 0 else acc_hi + hi * gk
    d = D_MODEL
    x_out[...] = x_ref[...] + mod_ref[:, 5 * d:6 * d] * jnp.concatenate([acc_lo, acc_hi], axis=1)


def _combine(xa, mod, gates, pos, y_sorted, tm, n_lat_tiles):
    b, t_all, d = xa.shape
    n_tiles = t_all // tm
    return pl.pallas_call(
        functools.partial(_combine_kernel, tm=tm, n_tiles=n_tiles),
        out_shape=jax.ShapeDtypeStruct((b, t_all, d), F32),
        grid=(b, n_tiles),
        in_specs=[
            pl.BlockSpec(memory_space=pl.ANY),
            pl.BlockSpec(memory_space=pl.ANY),
            pl.BlockSpec((None, tm, d), lambda bi, i: (bi, i, 0)),
            pl.BlockSpec((None, None, 1, 6 * d), lambda bi, i: (bi, jnp.where(i < n_lat_tiles, 0, 1), 0, 0)),
            pl.BlockSpec((None, tm, ROUTER_PAD), lambda bi, i: (bi, i, 0)),
        ],
        out_specs=pl.BlockSpec((None, tm, d), lambda bi, i: (bi, i, 0)),
        scratch_shapes=[pltpu.SMEM((2, tm * TOP_K), I32), pltpu.VMEM((2, tm * TOP_K, ROW_TILES, 128), U32),
                        pltpu.SemaphoreType.DMA((2,)), pltpu.SemaphoreType.DMA((2,))],
        input_output_aliases={2: 0},
        compiler_params=_cparams(("arbitrary", "arbitrary")),
        name="moe_combine",
    )(pos, y_sorted, xa, mod, gates)


def _final_kernel(x_ref, g_ref, o_ref):
    o_ref[...] = _rms(x_ref[...], g_ref[...])


def _final_norm(xa, g, seq, tm):
    b, _, d = xa.shape
    return pl.pallas_call(
        _final_kernel,
        out_shape=jax.ShapeDtypeStruct((b, seq, d), F32),
        grid=(b, seq // tm),
        in_specs=[pl.BlockSpec((None, tm, d), lambda bi, i: (bi, i, 0)),
                  pl.BlockSpec((1, d), lambda bi, i: (0, 0))],
        out_specs=pl.BlockSpec((None, tm, d), lambda bi, i: (bi, i, 0)),
        compiler_params=_cparams(("parallel", "parallel")),
        name="final_norm",
    )(xa, g.reshape(1, d))


def _pad_cols(w, width):
    return jnp.pad(w, ((0, 0), (0, width - w.shape[1])))


def _rope_partner(w):
    a = ROPE_AXIS // 2
    return jnp.concatenate([-w[:, a:2 * a], w[:, 0:a], -w[:, 3 * a:4 * a], w[:, 2 * a:3 * a]], axis=1)


def _layer_weights(l, w_in, mla_q_norm_g, mla_w_uq, mla_kv_norm_g, mla_w_ukv, gla_w_gate_f, gla_b_gate_f,
                   gla_w_gate_b, gla_b_gate_b, gla_norm_g, w_br_mla, w_br_fnet, w_br_gla, w_o, norm1_g,
                   norm2_g, router_w, router_b, exp_w_up, exp_b_up, exp_w_down, exp_b_down):
    d = D_MODEL
    splits = np.cumsum([MLA_Q_RANK, MLA_KV_RANK + MLA_ROPE, FNET_W, GLA_QK_W, GLA_QK_W, GLA_W, GLA_W,
                        GLA_GATE_RANK, GLA_GATE_RANK])
    wq, wkv, wfn, wgq, wgk, wgv, wog, wgf, wgb, wgate = jnp.split(w_in[l], [int(s) for s in splits], axis=1)
    w_in_p = jnp.concatenate([wq, _pad_cols(wkv, 256), wfn, wgq, wgk, wgv, wog,
                              _pad_cols(jnp.concatenate([wgf, wgb], axis=1), 128), wgate], axis=1).astype(BF16)

    qk = MLA_NOPE + MLA_ROPE
    zq = jnp.zeros((MLA_Q_RANK, HEAD_PAD - qk), F32)
    zn = jnp.zeros((MLA_Q_RANK, MLA_NOPE), F32)
    wqa, wqb = [], []
    for hd in range(MLA_HEADS):
        wh = mla_w_uq[l][:, hd * qk:(hd + 1) * qk]
        wqa.append(jnp.concatenate([wh, zq], axis=1))
        wqb.append(jnp.concatenate([zn, _rope_partner(wh[:, MLA_NOPE:]), zq], axis=1))
    eye = jnp.eye(MLA_ROPE, dtype=F32)
    wka, wkb, wv = [], [], []
    for hd in range(MLA_HEADS):
        wh = mla_w_ukv[l][:, hd * (MLA_NOPE + MLA_V):(hd + 1) * (MLA_NOPE + MLA_V)]
        top = jnp.concatenate([wh[:, 0:MLA_NOPE], jnp.zeros((MLA_KV_RANK, HEAD_PAD - MLA_NOPE), F32)], axis=1)
        mid_a = jnp.concatenate([jnp.zeros((MLA_ROPE, MLA_NOPE), F32), eye,
                                 jnp.zeros((MLA_ROPE, HEAD_PAD - qk), F32)], axis=1)
        mid_b = jnp.concatenate([jnp.zeros((MLA_ROPE, MLA_NOPE), F32), _rope_partner(eye),
                                 jnp.zeros((MLA_ROPE, HEAD_PAD - qk), F32)], axis=1)
        bot = jnp.zeros((256 - MLA_KV_RANK - MLA_ROPE, HEAD_PAD), F32)
        wka.append(jnp.concatenate([top, mid_a, bot], axis=0))
        wkb.append(jnp.concatenate([jnp.zeros_like(top), mid_b, bot], axis=0))
        wv.append(jnp.concatenate([wh[:, MLA_NOPE:], jnp.zeros((256 - MLA_KV_RANK, MLA_V), F32)], axis=0))

    ch = np.arange(FNET_GROUP_W)
    ang = 2.0 * np.pi * ((ch[:, None] * ch[None, :]) % FNET_GROUP_W) / FNET_GROUP_W
    wc = np.concatenate([np.cos(ang), np.sin(ang)], axis=1) / math.sqrt(FNET_GROUP_W)

    wg = jnp.zeros((128, 2 * GLA_QK_W), F32)
    wg = wg.at[0:GLA_GATE_RANK, 0:GLA_QK_W].set(gla_w_gate_f[l])
    wg = wg.at[GLA_GATE_RANK:2 * GLA_GATE_RANK, GLA_QK_W:].set(gla_w_gate_b[l])

    return {
        "norm1_g": norm1_g[l].reshape(1, d),
        "w_in": w_in_p,
        "q_norm_g": mla_q_norm_g[l].reshape(1, -1),
        "wqa": jnp.concatenate(wqa, axis=1).astype(BF16),
        "wqb": jnp.concatenate(wqb, axis=1).astype(BF16),
        "kv_norm_g": mla_kv_norm_g[l].reshape(1, -1),
        "wka": jnp.concatenate(wka, axis=1).astype(BF16),
        "wkb": jnp.concatenate(wkb, axis=1).astype(BF16),
        "wv": jnp.concatenate(wv, axis=1).astype(BF16),
        "wc": jnp.asarray(wc, BF16),
        "wg": wg.astype(BF16),
        "bg": jnp.concatenate([gla_b_gate_f[l], gla_b_gate_b[l]]).reshape(1, -1),
        "gla_norm_g": gla_norm_g[l].reshape(1, -1),
        "w_br_mla": w_br_mla[l].astype(BF16),
        "w_br_fnet": w_br_fnet[l].astype(BF16),
        "w_br_gla": w_br_gla[l].astype(BF16),
        "w_o": w_o[l].astype(BF16),
        "norm2_g": norm2_g[l].reshape(1, d),
        "router_w": _pad_cols(router_w[l], ROUTER_PAD).astype(BF16),
        "router_b": _pad_cols(router_b[l].reshape(1, -1), ROUTER_PAD),
        "w_up": exp_w_up[l].astype(BF16),
        "b_up": exp_b_up[l].reshape(N_EXPERTS, 1, -1),
        "w_down": exp_w_down[l].astype(BF16),
        "b_down": exp_b_down[l].reshape(N_EXPERTS, 1, -1),
    }


def _rope_tables(seq, ctx):
    rows = seq // GRID_W
    row = jnp.repeat(jnp.arange(rows, dtype=F32), GRID_W)
    col = jnp.tile(jnp.arange(GRID_W, dtype=F32), rows)
    inv_freq = ROPE_BASE ** (-jnp.arange(0, ROPE_AXIS, 2, dtype=F32) / ROPE_AXIS)
    ang_r = row[:, None] * inv_freq
    ang_c = col[:, None] * inv_freq
    ang = jnp.concatenate([ang_r, ang_r, ang_c, ang_c], axis=1)
    ang = jnp.concatenate([ang, jnp.zeros((ctx, MLA_ROPE), F32)], axis=0)
    t_all = seq + ctx
    pad = jnp.zeros((t_all, HEAD_PAD - MLA_NOPE - MLA_ROPE), F32)
    cos_t = jnp.concatenate([jnp.ones((t_all, MLA_NOPE), F32), jnp.cos(ang), pad], axis=1)
    sin_t = jnp.concatenate([jnp.zeros((t_all, MLA_NOPE), F32), jnp.sin(ang), pad], axis=1)
    return cos_t, sin_t


def _dft_mats(t):
    r = np.arange(t, dtype=np.int64)
    ang = 2.0 * np.pi * ((r[:, None] * r[None, :]) % t) / t
    s = 1.0 / math.sqrt(t)
    return jnp.asarray(np.cos(ang) * s, BF16), jnp.asarray(-np.sin(ang) * s, BF16)


def _routing(top_idx, n_tok):
    n_assign = n_tok * TOP_K
    flat_e = top_idx.reshape(n_assign)
    rt = 1024
    assert n_assign % rt == 0
    onehot = (flat_e[:, None] == jnp.arange(N_EXPERTS, dtype=I32)[None, :])
    oh3 = onehot.reshape(n_assign // rt, rt, N_EXPERTS).astype(BF16)
    tri = jnp.asarray(np.tril(np.ones((rt, rt), np.float32)), BF16)
    within = jnp.einsum("ts,nse->nte", tri, oh3, preferred_element_type=F32)
    tile_cnt = within[:, -1, :]
    tile_off = jnp.cumsum(tile_cnt, axis=0) - tile_cnt
    csum = (within + tile_off[:, None, :]).reshape(n_assign, N_EXPERTS)
    counts = (tile_off[-1] + tile_cnt[-1]).astype(I32)
    rank = jnp.sum(jnp.where(onehot, csum, 0.0), axis=1).astype(I32) - 1
    padded = ((counts + MOE_BLOCK - 1) // MOE_BLOCK) * MOE_BLOCK
    pad_end = jnp.cumsum(padded)
    pad_start = pad_end - padded
    dest = (pad_start[flat_e] + rank).astype(I32)
    n_blocks = -(-n_assign // MOE_BLOCK) + N_EXPERTS
    n_rows = n_blocks * MOE_BLOCK
    row_tok = jnp.zeros((n_rows,), I32).at[dest].set(jnp.arange(n_assign, dtype=I32) // TOP_K)
    blk_e = jnp.minimum(jnp.searchsorted(pad_end, jnp.arange(n_blocks, dtype=I32) * MOE_BLOCK, side="right"),
                        N_EXPERTS - 1).astype(I32)
    return blk_e, row_tok.reshape(n_blocks, MOE_BLOCK), dest


def kernel(x, c, ctx, c_ctx, w_mod, b_mod, norm1_g, w_in, mla_q_norm_g, mla_w_uq, mla_kv_norm_g, mla_w_ukv,
           gla_w_gate_f, gla_b_gate_f, gla_w_gate_b, gla_b_gate_b, gla_norm_g, w_br_mla, w_br_fnet, w_br_gla,
           w_o, norm2_g, router_w, router_b, exp_w_up, exp_b_up, exp_w_down, exp_b_down, final_norm_g):
    b, seq, d = x.shape
    n_ctx = ctx.shape[1]
    n_layers = w_mod.shape[0]
    t_all = seq + n_ctx
    assert d == D_MODEL and seq % n_ctx == 0 and seq % GRID_W == 0 and n_ctx % GLA_CHUNK == 0
    tm = min(256, n_ctx)
    assert n_ctx % tm == 0 and seq % tm == 0 and (tm * TOP_K) % MOE_BLOCK == 0
    n_lat_tiles = seq // tm
    t_fourier = min(512, seq)

    xa = jnp.concatenate([x, ctx], axis=1)

    rows = -(-(b + 1) // 8) * 8
    cv = jnp.zeros((rows, d), F32).at[0:b].set(c).at[b].set(c_ctx)
    mod_all = _mod_all(cv, w_mod, b_mod)
    cos_t, sin_t = _rope_tables(seq, n_ctx)
    cm_l, sm_l = _dft_mats(seq)
    cm_c, sm_c = _dft_mats(n_ctx)

    n_tok = b * t_all
    for l in range(n_layers):
        lw = _layer_weights(l, w_in, mla_q_norm_g, mla_w_uq, mla_kv_norm_g, mla_w_ukv, gla_w_gate_f,
                            gla_b_gate_f, gla_w_gate_b, gla_b_gate_b, gla_norm_g, w_br_mla, w_br_fnet,
                            w_br_gla, w_o, norm1_g, norm2_g, router_w, router_b, exp_w_up, exp_b_up,
                            exp_w_down, exp_b_down)
        mod_lat = mod_all[l, 0:b]
        mod_ctx = jnp.broadcast_to(mod_all[l, b][None, :], (b, 6 * d))
        mod = jnp.stack([mod_lat, mod_ctx], axis=1).reshape(b, 2, 1, 6 * d)

        qc, kc, v, pq, gq, gk, gv, lfb, og, gate = _inproj(xa, mod, cos_t, sin_t, lw, tm, n_lat_tiles)
        o_mla = _attention(qc, kc, v, tm, seq, n_ctx)
        y_fn_lat = _fourier(pq, cm_l, sm_l, t_fourier, 0)
        y_fn_ctx = _fourier(pq, cm_c, sm_c, n_ctx, seq)
        o_f, o_b = _gla(gq, gk, gv, lfb, tm, seq, n_ctx)
        xa, h2, top_idx, gates = _merge(xa, mod, o_mla, y_fn_lat, y_fn_ctx, o_f, o_b, og, gate, lw, tm,
                                        n_lat_tiles)

        blk_e, row_tok, dest = _routing(top_idx[..., 0:TOP_K], n_tok)
        y_sorted = _moe_experts(blk_e, row_tok, h2.reshape(n_tok, ROW_TILES, 128), lw)
        pos = dest.reshape(n_tok // tm, tm, TOP_K).transpose(0, 2, 1).reshape(n_tok // tm, tm * TOP_K)
        xa = _combine(xa, mod, gates, pos, y_sorted, tm, n_lat_tiles)

    return _final_norm(xa, final_norm_g, seq, tm)
```

```python
import functools
import math

import jax
import jax.numpy as jnp
import numpy as np
from jax import lax
from jax.experimental import pallas as pl
from jax.experimental.pallas import tpu as pltpu

F32 = jnp.float32
BF16 = jnp.bfloat16
I32 = jnp.int32
U32 = jnp.uint32

D_MODEL = 1024
GRID_W = 64
EPS = 1e-6

MLA_HEADS = 8
MLA_Q_RANK = 256
MLA_KV_RANK = 128
MLA_NOPE = 64
MLA_ROPE = 32
MLA_V = 64
MLA_SCALE = (MLA_NOPE + MLA_ROPE) ** -0.5
ROPE_AXIS = MLA_ROPE // 2
ROPE_BASE = 10000.0
HEAD_PAD = 128

FNET_GROUPS = 4
FNET_GROUP_W = 128
FNET_W = FNET_GROUPS * FNET_GROUP_W

GLA_HEADS = 4
GLA_DK = 64
GLA_DV = 128
GLA_GATE_RANK = 16
GLA_TAU = 16.0
GLA_CHUNK = 64
GLA_SUB = 16
GLA_SCALE = GLA_DK ** -0.5
GLA_QK_W = GLA_HEADS * GLA_DK
GLA_W = GLA_HEADS * GLA_DV

N_EXPERTS = 32
TOP_K = 4
D_EXPERT = D_MODEL
SWIGLU_ALPHA = 1.702
SWIGLU_LIMIT = 7.0
MOE_BLOCK = 256
ROUTER_PAD = 128

MLA_W = MLA_HEADS * MLA_V

VMEM_LIMIT_V7X = 56 * 1024 * 1024

C_Q = 0
C_KV = C_Q + MLA_Q_RANK
C_FN = C_KV + 256
C_GQ = C_FN + FNET_W
C_GK = C_GQ + GLA_QK_W
C_GV = C_GK + GLA_QK_W
C_OG = C_GV + GLA_W
C_GG = C_OG + GLA_W
C_GATE = C_GG + 128
IN_WIDTH_PAD = C_GATE + 3 * D_MODEL


def _cparams(sem, vmem=VMEM_LIMIT_V7X):
    return pltpu.CompilerParams(dimension_semantics=sem, vmem_limit_bytes=vmem)


def _rms(x, g):
    return x * lax.rsqrt(jnp.mean(x * x, axis=-1, keepdims=True) + EPS) * g


def _dot(a, b):
    return jnp.dot(a, b, preferred_element_type=F32)


def _dot_nt(a, b):
    return lax.dot_general(a, b, (((1,), (1,)), ((), ())), preferred_element_type=F32)


def _dot_tn(a, b):
    return lax.dot_general(a, b, (((0,), (0,)), ((), ())), preferred_element_type=F32)


def _pack_bf16_pairs(x):
    w = x.shape[1] // 2
    r = lax.bitcast_convert_type(x.astype(BF16).astype(F32), U32)
    return r[:, w:] | (r[:, :w] >> 16)


def _unpack_bf16_pairs(u):
    lo = lax.bitcast_convert_type(u << 16, F32)
    hi = lax.bitcast_convert_type(u & jnp.uint32(0xFFFF0000), F32)
    return lo, hi


ROW_TILES = D_MODEL // 2 // 128


def _store_row_tiles(ref, x):
    for j in range(ROW_TILES):
        ref[:, j, :] = x[:, j * 128:(j + 1) * 128]


def _load_row_tiles(ref):
    return jnp.concatenate([ref[:, j, :] for j in range(ROW_TILES)], axis=1)


def _mod_kernel(cv_ref, w_ref, b_ref, o_ref):
    cv = cv_ref[...]
    s = cv * jax.nn.sigmoid(cv)
    o_ref[...] = jnp.dot(s, w_ref[...], preferred_element_type=F32,
                         precision=lax.Precision.HIGHEST) + b_ref[...]


def _mod_all(cv, w_mod, b_mod):
    n_layers, d, w6 = w_mod.shape
    rows = cv.shape[0]
    tn = 1536
    return pl.pallas_call(
        _mod_kernel,
        out_shape=jax.ShapeDtypeStruct((n_layers, rows, w6), F32),
        grid=(n_layers, w6 // tn),
        in_specs=[
            pl.BlockSpec((rows, d), lambda l, j: (0, 0)),
            pl.BlockSpec((None, d, tn), lambda l, j: (l, 0, j)),
            pl.BlockSpec((None, 1, tn), lambda l, j: (l, 0, j)),
        ],
        out_specs=pl.BlockSpec((None, rows, tn), lambda l, j: (l, 0, j)),
        compiler_params=_cparams(("parallel", "parallel")),
        name="adaln_mod",
    )(cv, w_mod, b_mod.reshape(n_layers, 1, w6))


def _inproj_kernel(x_ref, mod_ref, cos_ref, sin_ref, g1_ref, win_ref, qng_ref, wqa_ref, wqb_ref,
                   kvg_ref, wka_ref, wkb_ref, wv_ref, wc_ref, wg_ref, bg_ref,
                   q_out, k_out, v_out, pq_out, gq_out, gk_out, gv_out, lfb_out, og_out, gate_out):
    d = D_MODEL
    x = x_ref[...]
    m = mod_ref[...]
    h = (_rms(x, g1_ref[...]) * (1.0 + m[:, d:2 * d]) + m[:, 0:d]).astype(BF16)

    def proj(a, b):
        return _dot(h, win_ref[:, a:b])

    cos = jnp.tile(cos_ref[...], (1, MLA_HEADS))
    sin = jnp.tile(sin_ref[...], (1, MLA_HEADS))

    nq = _rms(proj(C_Q, C_KV), qng_ref[...]).astype(BF16)
    q = (_dot(nq, wqa_ref[...]) * cos + _dot(nq, wqb_ref[...]) * sin) * MLA_SCALE
    q_out[...] = q.astype(BF16)
    ukv = proj(C_KV, C_FN)
    ckn = _rms(ukv[:, 0:MLA_KV_RANK], kvg_ref[...])
    lhs = jnp.concatenate([ckn, ukv[:, MLA_KV_RANK:]], axis=1).astype(BF16)
    k_out[...] = (_dot(lhs, wka_ref[...]) * cos + _dot(lhs, wkb_ref[...]) * sin).astype(BF16)
    v_out[...] = _dot(lhs, wv_ref[...]).astype(BF16)

    ufn = proj(C_FN, C_GQ).astype(BF16)
    ps, qs = [], []
    for g in range(FNET_GROUPS):
        r = _dot(ufn[:, g * FNET_GROUP_W:(g + 1) * FNET_GROUP_W], wc_ref[...])
        ps.append(r[:, 0:FNET_GROUP_W])
        qs.append(r[:, FNET_GROUP_W:])
    pq_out[...] = jnp.concatenate(ps + qs, axis=1).astype(BF16)

    gq_out[...] = (proj(C_GQ, C_GK) * GLA_SCALE).astype(BF16)
    gk_out[...] = proj(C_GK, C_GV).astype(BF16)
    gv_out[...] = proj(C_GV, C_OG).astype(BF16)
    og_out[...] = proj(C_OG, C_GG).astype(BF16)
    z = _dot(proj(C_GG, C_GATE).astype(BF16), wg_ref[...]) + bg_ref[...]
    lfb_out[...] = (jnp.minimum(z, 0.0) - jnp.log1p(jnp.exp(-jnp.abs(z)))) * (1.0 / GLA_TAU)
    gate_out[...] = proj(C_GATE, IN_WIDTH_PAD).astype(BF16)


def _inproj(xa, mod, cos_t, sin_t, lw, tm, n_lat_tiles):
    b, t_all, d = xa.shape
    nt = t_all // tm

    def tok(w, dt):
        return jax.ShapeDtypeStruct((b, t_all, w), dt)

    def tspec(w):
        return pl.BlockSpec((None, tm, w), lambda bi, i: (bi, i, 0))

    def cspec(a):
        return pl.BlockSpec(a.shape, lambda bi, i: (0,) * a.ndim)

    consts = [lw["norm1_g"], lw["w_in"], lw["q_norm_g"], lw["wqa"], lw["wqb"], lw["kv_norm_g"],
              lw["wka"], lw["wkb"], lw["wv"], lw["wc"], lw["wg"], lw["bg"]]
    widths = [(1024, BF16), (1024, BF16), (MLA_W, BF16), (2 * FNET_W, BF16), (GLA_QK_W, BF16),
              (GLA_QK_W, BF16), (GLA_W, BF16), (2 * GLA_QK_W, F32), (GLA_W, BF16), (3 * D_MODEL, BF16)]
    return pl.pallas_call(
        _inproj_kernel,
        out_shape=[tok(w, dt) for w, dt in widths],
        grid=(b, nt),
        in_specs=[
            tspec(d),
            pl.BlockSpec((None, None, 1, 6 * d), lambda bi, i: (bi, jnp.where(i < n_lat_tiles, 0, 1), 0, 0)),
            pl.BlockSpec((tm, HEAD_PAD), lambda bi, i: (i, 0)),
            pl.BlockSpec((tm, HEAD_PAD), lambda bi, i: (i, 0)),
        ] + [cspec(a) for a in consts],
        out_specs=[tspec(w) for w, _ in widths],
        compiler_params=_cparams(("parallel", "parallel")),
        name="inproj",
    )(xa, mod, cos_t, sin_t, *consts)


def _attn_kernel(q_ref, k_ref, v_ref, o_ref):
    lane = lax.broadcasted_iota(I32, (q_ref.shape[0], 2 * MLA_V), 1)
    outs = []
    for hp in range(MLA_HEADS // 2):
        vp = v_ref[:, hp * 2 * MLA_V:(hp + 1) * 2 * MLA_V]
        rs = []
        for j in range(2):
            hd = 2 * hp + j
            q = q_ref[:, hd * HEAD_PAD:(hd + 1) * HEAD_PAD]
            k = k_ref[:, hd * HEAD_PAD:(hd + 1) * HEAD_PAD]
            s = _dot_nt(q, k)
            p = jnp.exp(s - jnp.max(s, axis=-1, keepdims=True))
            l = jnp.sum(p, axis=-1, keepdims=True)
            rs.append(_dot(p.astype(BF16), vp) / l)
        outs.append(jnp.where(lane < MLA_V, rs[0], rs[1]))
    o_ref[...] = jnp.concatenate(outs, axis=1).astype(BF16)


def _attention(qc, kc, v, tq, q_row0, n_q, k_row0, n_k):
    b = qc.shape[0]
    assert q_row0 % tq == 0 and n_q % tq == 0 and k_row0 % n_k == 0
    return pl.pallas_call(
        _attn_kernel,
        out_shape=jax.ShapeDtypeStruct((b, n_q, MLA_W), BF16),
        grid=(b, n_q // tq),
        in_specs=[
            pl.BlockSpec((None, tq, MLA_HEADS * HEAD_PAD), lambda bi, i: (bi, q_row0 // tq + i, 0)),
            pl.BlockSpec((None, n_k, MLA_HEADS * HEAD_PAD), lambda bi, i: (bi, k_row0 // n_k, 0),
                         pipeline_mode=pl.Buffered(1)),
            pl.BlockSpec((None, n_k, MLA_W), lambda bi, i: (bi, k_row0 // n_k, 0),
                         pipeline_mode=pl.Buffered(1)),
        ],
        out_specs=pl.BlockSpec((None, tq, MLA_W), lambda bi, i: (bi, i, 0)),
        compiler_params=_cparams(("parallel", "parallel")),
        name="mla_attention",
    )(qc, kc, v)


def _fourier_kernel(cm_ref, sm_ref, pq_ref, o_ref):
    y = _dot(cm_ref[...], pq_ref[:, 0:FNET_W]) + _dot(sm_ref[...], pq_ref[:, FNET_W:])
    o_ref[...] = y.astype(BF16)


def _fourier(pq, cm, sm, tm, row0):
    b = pq.shape[0]
    t = cm.shape[0]
    return pl.pallas_call(
        _fourier_kernel,
        out_shape=jax.ShapeDtypeStruct((b, t, FNET_W), BF16),
        grid=(t // tm, b),
        in_specs=[
            pl.BlockSpec((tm, t), lambda i, bi: (i, 0)),
            pl.BlockSpec((tm, t), lambda i, bi: (i, 0)),
            pl.BlockSpec((None, t, 2 * FNET_W), lambda i, bi: (bi, row0 // t, 0)),
        ],
        out_specs=pl.BlockSpec((None, tm, FNET_W), lambda i, bi: (bi, i, 0)),
        compiler_params=_cparams(("parallel", "parallel")),
        name="fourier_mix",
    )(cm, sm, pq)


def _gla_chunk(q, k, v, f, tri, ind_ref, dmask_ref, st_ref, reverse):
    c = GLA_CHUNK
    f0 = f.astype(BF16)
    r1 = f - f0.astype(F32)
    f1 = r1.astype(BF16)
    f2 = (r1 - f1.astype(F32)).astype(BF16)
    g = _dot(tri, f0) + _dot(tri, f1) + _dot(tri, f2)
    e = g[0:1, :] if reverse else g[c - 1:c, :]
    qf = q.astype(F32)
    kf = k.astype(F32)
    qg = (qf * jnp.exp(g)).astype(BF16)
    kd = (kf * jnp.exp(e - g)).astype(BF16)
    st = st_ref[...]
    st_b = st.astype(BF16)

    sb = GLA_SUB
    ind = ind_ref[...]
    row_sb = lax.broadcasted_iota(I32, (sb, GLA_QK_W), 0)
    row_c = lax.broadcasted_iota(I32, (c, GLA_QK_W), 0)
    a_rows = []
    for i in range(c // sb):
        lo = i * sb
        gb, kb, qb = g[lo:lo + sb], kf[lo:lo + sb], qf[lo:lo + sb]
        ws = []
        for t in range(sb):
            w = jnp.exp(jnp.minimum(gb[t:t + 1] - gb, 0.0)) * kb * qb[t:t + 1]
            keep = (row_sb >= t) if reverse else (row_sb <= t)
            ws.append(jnp.where(keep, w, 0.0))
        w_all = jnp.concatenate(ws, axis=0).astype(BF16)
        r = _dot(w_all, ind) * dmask_ref[i]
        a_i = jnp.sum(r.reshape(sb, sb, GLA_QK_W), axis=1)
        ref_row = lo + sb if reverse else lo - 1
        if 0 <= ref_row < c:
            gr = g[ref_row:ref_row + 1]
            qt = (qb * jnp.exp(gb - gr)).astype(BF16)
            valid = (row_c >= lo + sb) if reverse else (row_c < lo)
            kt = jnp.where(valid, kf * jnp.exp(jnp.minimum(gr - g, 0.0)), 0.0).astype(BF16)
            kbd = jnp.tile(kt, (GLA_HEADS, 1)) * ind
            a_i = a_i + _dot_nt(qt, kbd)
        a_rows.append(a_i)
    a = jnp.concatenate(a_rows, axis=0).astype(BF16)

    outs, upd = [], []
    for hd in range(GLA_HEADS):
        ks_ = slice(hd * GLA_DK, (hd + 1) * GLA_DK)
        vh = v[:, hd * GLA_DV:(hd + 1) * GLA_DV]
        o_inter = _dot_nt(qg[:, ks_], st_b[:, ks_])
        o_intra = _dot(a[:, ks_], vh)
        outs.append(o_inter + o_intra)
        upd.append(_dot_tn(vh, kd[:, ks_]))
    st_ref[...] = st * jnp.exp(e) + jnp.concatenate(upd, axis=1)
    return jnp.concatenate(outs, axis=1)


def _gla_kernel(qf_ref, kf_ref, vf_ref, ff_ref, qb_ref, kb_ref, vb_ref, fb_ref,
                tril_ref, triu_ref, ind_ref, dmask_ref, of_ref, ob_ref,
                stf, stb, *, n_chunks):
    @pl.when(pl.program_id(1) == 0)
    def _():
        stf[...] = jnp.zeros_like(stf)
        stb[...] = jnp.zeros_like(stb)

    c = GLA_CHUNK
    for j in range(n_chunks):
        rows = slice(j * c, (j + 1) * c)
        of_ref[rows, :] = _gla_chunk(qf_ref[rows, :], kf_ref[rows, :], vf_ref[rows, :], ff_ref[rows, :],
                                     tril_ref[...], ind_ref, dmask_ref, stf, False)
    for j in reversed(range(n_chunks)):
        rows = slice(j * c, (j + 1) * c)
        ob_ref[rows, :] = _gla_chunk(qb_ref[rows, :], kb_ref[rows, :], vb_ref[rows, :], fb_ref[rows, :],
                                     triu_ref[...], ind_ref, dmask_ref, stb, True)


def _gla(gq, gk, gv, lfb, tb, seq, ctx):
    b, t_all, _ = gq.shape
    nlb, ncb = seq // tb, ctx // tb
    c = GLA_CHUNK

    def fwd(i):
        return jnp.where(i < ncb, nlb + i, i - ncb)

    def bwd(i):
        return jnp.where(i < ncb, nlb + ncb - 1 - i, nlb - 1 - (i - ncb))

    def spec(w, order, col=0):
        return pl.BlockSpec((None, tb, w), lambda bi, i: (bi, order(i), col))

    r = np.arange(c)
    tril = jnp.asarray(r[:, None] >= r[None, :], BF16)
    triu = jnp.asarray(r[:, None] <= r[None, :], BF16)
    hs = np.arange(GLA_QK_W)
    ind = jnp.asarray(hs[:, None] // GLA_DK == hs[None, :] // GLA_DK, BF16)
    sb = GLA_SUB
    s_loc = np.tile(np.arange(sb), sb)
    dmask = jnp.asarray(np.stack([(i * sb + s_loc)[:, None] == (hs % GLA_DK)[None, :] for i in range(c // sb)]),
                        F32)

    def cspec(a):
        return pl.BlockSpec(a.shape, lambda bi, i: (0,) * a.ndim)

    return pl.pallas_call(
        functools.partial(_gla_kernel, n_chunks=tb // c),
        out_shape=[jax.ShapeDtypeStruct((b, t_all, GLA_W), F32)] * 2,
        grid=(b, nlb + ncb),
        in_specs=[spec(GLA_QK_W, fwd), spec(GLA_QK_W, fwd), spec(GLA_W, fwd), spec(GLA_QK_W, fwd, 0),
                  spec(GLA_QK_W, bwd), spec(GLA_QK_W, bwd), spec(GLA_W, bwd), spec(GLA_QK_W, bwd, 1),
                  cspec(tril), cspec(triu), cspec(ind), cspec(dmask)],
        out_specs=[spec(GLA_W, fwd), spec(GLA_W, bwd)],
        scratch_shapes=[pltpu.VMEM((GLA_DV, GLA_QK_W), F32), pltpu.VMEM((GLA_DV, GLA_QK_W), F32)],
        compiler_params=_cparams(("parallel", "arbitrary")),
        name="gla_scan",
    )(gq, gk, gv, lfb, gq, gk, gv, lfb, tril, triu, ind, dmask)


def _merge_kernel(x_ref, mod_ref, oml_ref, omc_ref, yfl_ref, yfc_ref, of_ref, ob_ref, og_ref, gate_ref,
                  gng_ref, wbm_ref, wbf_ref, wbg_ref, wo_ref, g2_ref, rw_ref, rb_ref,
                  x_out, h_out, idx_out, gw_out, *, n_lat_tiles):
    d = D_MODEL
    m = mod_ref[...]
    is_lat = pl.program_id(1) < n_lat_tiles
    y_fn = jnp.where(is_lat, yfl_ref[...], yfc_ref[...])
    o_mla = jnp.where(is_lat, oml_ref[...], omc_ref[...])
    o = of_ref[...] + ob_ref[...]
    gng = gng_ref[...]
    og = og_ref[...].astype(F32)
    parts = []
    for hd in range(GLA_HEADS):
        sl = slice(hd * GLA_DV, (hd + 1) * GLA_DV)
        parts.append(_rms(o[:, sl], gng[:, sl]))
    o_gla = (jnp.concatenate(parts, axis=1) * (og * jax.nn.sigmoid(og))).astype(BF16)

    gate = jax.nn.sigmoid(gate_ref[...].astype(F32))
    y = (gate[:, 0:d] * _dot(o_mla, wbm_ref[...])
         + gate[:, d:2 * d] * _dot(y_fn, wbf_ref[...])
         + gate[:, 2 * d:3 * d] * _dot(o_gla, wbg_ref[...]))
    x1 = x_ref[...] + m[:, 2 * d:3 * d] * _dot(y.astype(BF16), wo_ref[...])
    x_out[...] = x1
    h2 = _rms(x1, g2_ref[...]) * (1.0 + m[:, 4 * d:5 * d]) + m[:, 3 * d:4 * d]
    _store_row_tiles(h_out, _pack_bf16_pairs(h2))

    logits = _dot(h2.astype(BF16), rw_ref[...]) + rb_ref[...]
    lane = lax.broadcasted_iota(I32, logits.shape, 1)
    lane_f = lane.astype(F32)
    neg = jnp.float32(-jnp.inf)
    lg = jnp.where(lane < N_EXPERTS, logits, neg)
    idx_acc = jnp.zeros(logits.shape, I32)
    val_acc = jnp.zeros(logits.shape, F32)
    v0 = None
    for kk in range(TOP_K):
        mx = jnp.max(lg, axis=-1, keepdims=True)
        ix = jnp.min(jnp.where(lg == mx, lane_f, float(ROUTER_PAD)), axis=-1, keepdims=True).astype(I32)
        if kk == 0:
            v0 = mx
        idx_acc = jnp.where(lane == kk, ix, idx_acc)
        val_acc = jnp.where(lane == kk, jnp.exp(mx - v0), val_acc)
        lg = jnp.where(lane == ix, neg, lg)
    idx_out[...] = idx_acc
    gw_out[...] = val_acc / jnp.sum(val_acc, axis=-1, keepdims=True)


def _merge(xa, mod, o_mla_lat, o_mla_ctx, y_fn_lat, y_fn_ctx, o_f, o_b, og, gate, lw, tm, n_lat_tiles):
    b, t_all, d = xa.shape

    def tspec(w):
        return pl.BlockSpec((None, tm, w), lambda bi, i: (bi, i, 0))

    def lat_spec(w):
        return pl.BlockSpec((None, tm, w), lambda bi, i: (bi, jnp.minimum(i, n_lat_tiles - 1), 0))

    def ctx_spec(w):
        return pl.BlockSpec((None, tm, w), lambda bi, i: (bi, jnp.maximum(i - n_lat_tiles, 0), 0))

    def cspec(a):
        return pl.BlockSpec(a.shape, lambda bi, i: (0,) * a.ndim)

    consts = [lw["gla_norm_g"], lw["w_br_mla"], lw["w_br_fnet"], lw["w_br_gla"], lw["w_o"],
              lw["norm2_g"], lw["router_w"], lw["router_b"]]
    return pl.pallas_call(
        functools.partial(_merge_kernel, n_lat_tiles=n_lat_tiles),
        out_shape=[jax.ShapeDtypeStruct((b, t_all, d), F32), jax.ShapeDtypeStruct((b, t_all, ROW_TILES, 128), U32),
                   jax.ShapeDtypeStruct((b, t_all, ROUTER_PAD), I32),
                   jax.ShapeDtypeStruct((b, t_all, ROUTER_PAD), F32)],
        grid=(b, t_all // tm),
        in_specs=[
            tspec(d),
            pl.BlockSpec((None, None, 1, 6 * d), lambda bi, i: (bi, jnp.where(i < n_lat_tiles, 0, 1), 0, 0)),
            lat_spec(MLA_W), ctx_spec(MLA_W), lat_spec(FNET_W), ctx_spec(FNET_W),
            tspec(GLA_W), tspec(GLA_W), tspec(GLA_W), tspec(3 * d),
        ] + [cspec(a) for a in consts],
        out_specs=[tspec(d), pl.BlockSpec((None, tm, ROW_TILES, 128), lambda bi, i: (bi, i, 0, 0)),
                   tspec(ROUTER_PAD), tspec(ROUTER_PAD)],
        input_output_aliases={0: 0},
        compiler_params=_cparams(("parallel", "parallel")),
        name="merge_router",
    )(xa, mod, o_mla_lat, o_mla_ctx, y_fn_lat, y_fn_ctx, o_f, o_b, og, gate, *consts)


GATHER_UNROLL = 8


def _row_gather(idx_smem, slot, src_hbm, dst, sem, n_rows, straight_line=False):
    def body(r, carry):
        pltpu.make_async_copy(src_hbm.at[pl.ds(idx_smem[slot, r], 1)], dst.at[slot, pl.ds(r, 1)],
                              sem.at[slot]).start()
        return carry

    if straight_line:
        for r in range(n_rows):
            body(r, 0)
    else:
        lax.fori_loop(0, n_rows, body, 0, unroll=GATHER_UNROLL)


def _moe_kernel(blk_e_ref, tok_hbm, h_hbm, wu_ref, bu_ref, wd_ref, bd_ref, y_out,
                idx_smem, xbuf, isem, gsem):
    i = pl.program_id(0)
    n = pl.num_programs(0)
    slot = lax.rem(i, 2)
    nxt = 1 - slot

    def idx_copy(blk, s):
        return pltpu.make_async_copy(tok_hbm.at[blk], idx_smem.at[s], isem.at[s])

    def rows_wait(s):
        pltpu.make_async_copy(xbuf.at[s], xbuf.at[s], gsem.at[s]).wait()

    last = n - 1

    @pl.when(i == 0)
    def _():
        idx_copy(0, 0).start()
        idx_copy(0, 0).wait()
        _row_gather(idx_smem, 0, h_hbm, xbuf, gsem, MOE_BLOCK)
        idx_copy(jnp.minimum(1, last), 1).start()

    idx_copy(jnp.minimum(i + 1, last), nxt).wait()
    rows_wait(slot)
    _row_gather(idx_smem, nxt, h_hbm, xbuf, gsem, MOE_BLOCK, straight_line=True)
    idx_copy(jnp.minimum(i + 2, last), slot).start()

    @pl.when(i == last)
    def _():
        rows_wait(nxt)
        idx_copy(last, slot).wait()

    xb = jnp.concatenate(_unpack_bf16_pairs(_load_row_tiles(xbuf.at[slot])), axis=1).astype(BF16)
    up = _dot(xb, wu_ref[...]) + bu_ref[...]
    glu = jnp.minimum(up[:, 0:D_EXPERT], SWIGLU_LIMIT)
    lin = jnp.clip(up[:, D_EXPERT:], -SWIGLU_LIMIT, SWIGLU_LIMIT)
    act = glu * jax.nn.sigmoid(SWIGLU_ALPHA * glu) * (lin + 1.0)
    yb = _dot(act.astype(BF16), wd_ref[...]) + bd_ref[...]
    _store_row_tiles(y_out, _pack_bf16_pairs(yb))


def _moe_experts(blk_e, row_tok, h_flat, lw):
    n_blocks = blk_e.shape[0]
    d = D_MODEL
    grid_spec = pltpu.PrefetchScalarGridSpec(
        num_scalar_prefetch=1,
        grid=(n_blocks,),
        in_specs=[
            pl.BlockSpec(memory_space=pl.ANY),
            pl.BlockSpec(memory_space=pl.ANY),
            pl.BlockSpec((None, d, 2 * D_EXPERT), lambda i, be: (be[i], 0, 0)),
            pl.BlockSpec((None, 1, 2 * D_EXPERT), lambda i, be: (be[i], 0, 0)),
            pl.BlockSpec((None, D_EXPERT, d), lambda i, be: (be[i], 0, 0)),
            pl.BlockSpec((None, 1, d), lambda i, be: (be[i], 0, 0)),
        ],
        out_specs=pl.BlockSpec((MOE_BLOCK, ROW_TILES, 128), lambda i, be: (i, 0, 0)),
        scratch_shapes=[pltpu.SMEM((2, MOE_BLOCK), I32), pltpu.VMEM((2, MOE_BLOCK, ROW_TILES, 128), U32),
                        pltpu.SemaphoreType.DMA((2,)), pltpu.SemaphoreType.DMA((2,))],
    )
    return pl.pallas_call(
        _moe_kernel,
        out_shape=jax.ShapeDtypeStruct((n_blocks * MOE_BLOCK, ROW_TILES, 128), U32),
        grid_spec=grid_spec,
        compiler_params=_cparams(("arbitrary",)),
        name="moe_experts",
    )(blk_e, row_tok, h_flat, lw["w_up"], lw["b_up"], lw["w_down"], lw["b_down"])


def _combine_kernel(pos_hbm, y_hbm, x_ref, mod_ref, gw_ref, x_out, idx_smem, ybuf, isem, gsem, *, tm, n_tiles):
    bi = pl.program_id(0)
    ti = pl.program_id(1)
    i = bi * n_tiles + ti
    n = pl.num_programs(0) * n_tiles
    slot = lax.rem(i, 2)
    nxt = 1 - slot
    n_rows = tm * TOP_K

    def idx_copy(blk, s):
        return pltpu.make_async_copy(pos_hbm.at[blk], idx_smem.at[s], isem.at[s])

    @pl.when(i == 0)
    def _():
        idx_copy(0, 0).start()
        idx_copy(0, 0).wait()
        _row_gather(idx_smem, 0, y_hbm, ybuf, gsem, n_rows)

        @pl.when(n > 1)
        def _():
            idx_copy(1, 1).start()

    @pl.when(i + 1 < n)
    def _():
        idx_copy(i + 1, nxt).wait()
        _row_gather(idx_smem, nxt, y_hbm, ybuf, gsem, n_rows)

    @pl.when(i + 2 < n)
    def _():
        idx_copy(i + 2, slot).start()

    pltpu.make_async_copy(ybuf.at[slot], ybuf.at[slot], gsem.at[slot]).wait()
    gw = gw_ref[...]
    acc_lo = acc_hi = None
    for kk in range(TOP_K):
        lo, hi = _unpack_bf16_pairs(_load_row_tiles(ybuf.at[slot, pl.ds(kk * tm, tm)]))
        gk = gw[:, kk:kk + 1]
        acc_lo = lo * gk if kk == 0 else acc_lo + lo * gk
        acc_hi = hi * gk if kk == 0 else acc_hi + hi * gk
    d = D_MODEL
    x_out[...] = x_ref[...] + mod_ref[:, 5 * d:6 * d] * jnp.concatenate([acc_lo, acc_hi], axis=1)


def _combine(xa, mod, gates, pos, y_sorted, tm, n_lat_tiles):
    b, t_all, d = xa.shape
    n_tiles = t_all // tm
    return pl.pallas_call(
        functools.partial(_combine_kernel, tm=tm, n_tiles=n_tiles),
        out_shape=jax.ShapeDtypeStruct((b, t_all, d), F32),
        grid=(b, n_tiles),
        in_specs=[
            pl.BlockSpec(memory_space=pl.ANY),
            pl.BlockSpec(memory_space=pl.ANY),
            pl.BlockSpec((None, tm, d), lambda bi, i: (bi, i, 0)),
            pl.BlockSpec((None, None, 1, 6 * d), lambda bi, i: (bi, jnp.where(i < n_lat_tiles, 0, 1), 0, 0)),
            pl.BlockSpec((None, tm, ROUTER_PAD), lambda bi, i: (bi, i, 0)),
        ],
        out_specs=pl.BlockSpec((None, tm, d), lambda bi, i: (bi, i, 0)),
        scratch_shapes=[pltpu.SMEM((2, tm * TOP_K), I32), pltpu.VMEM((2, tm * TOP_K, ROW_TILES, 128), U32),
                        pltpu.SemaphoreType.DMA((2,)), pltpu.SemaphoreType.DMA((2,))],
        input_output_aliases={2: 0},
        compiler_params=_cparams(("arbitrary", "arbitrary")),
        name="moe_combine",
    )(pos, y_sorted, xa, mod, gates)


def _final_kernel(x_ref, g_ref, o_ref):
    o_ref[...] = _rms(x_ref[...], g_ref[...])


def _final_norm(xa, g, seq, tm):
    b, _, d = xa.shape
    return pl.pallas_call(
        _final_kernel,
        out_shape=jax.ShapeDtypeStruct((b, seq, d), F32),
        grid=(b, seq // tm),
        in_specs=[pl.BlockSpec((None, tm, d), lambda bi, i: (bi, i, 0)),
                  pl.BlockSpec((1, d), lambda bi, i: (0, 0))],
        out_specs=pl.BlockSpec((None, tm, d), lambda bi, i: (bi, i, 0)),
        compiler_params=_cparams(("parallel", "parallel")),
        name="final_norm",
    )(xa, g.reshape(1, d))


def _pad_cols(w, width):
    return jnp.pad(w, ((0, 0), (0, width - w.shape[1])))


def _rope_partner(w):
    a = ROPE_AXIS // 2
    return jnp.concatenate([-w[:, a:2 * a], w[:, 0:a], -w[:, 3 * a:4 * a], w[:, 2 * a:3 * a]], axis=1)


def _layer_weights(l, w_in, mla_q_norm_g, mla_w_uq, mla_kv_norm_g, mla_w_ukv, gla_w_gate_f, gla_b_gate_f,
                   gla_w_gate_b, gla_b_gate_b, gla_norm_g, w_br_mla, w_br_fnet, w_br_gla, w_o, norm1_g,
                   norm2_g, router_w, router_b, exp_w_up, exp_b_up, exp_w_down, exp_b_down):
    d = D_MODEL
    splits = np.cumsum([MLA_Q_RANK, MLA_KV_RANK + MLA_ROPE, FNET_W, GLA_QK_W, GLA_QK_W, GLA_W, GLA_W,
                        GLA_GATE_RANK, GLA_GATE_RANK])
    wq, wkv, wfn, wgq, wgk, wgv, wog, wgf, wgb, wgate = jnp.split(w_in[l], [int(s) for s in splits], axis=1)
    w_in_p = jnp.concatenate([wq, _pad_cols(wkv, 256), wfn, wgq, wgk, wgv, wog,
                              _pad_cols(jnp.concatenate([wgf, wgb], axis=1), 128), wgate], axis=1).astype(BF16)

    qk = MLA_NOPE + MLA_ROPE
    zq = jnp.zeros((MLA_Q_RANK, HEAD_PAD - qk), F32)
    zn = jnp.zeros((MLA_Q_RANK, MLA_NOPE), F32)
    wqa, wqb = [], []
    for hd in range(MLA_HEADS):
        wh = mla_w_uq[l][:, hd * qk:(hd + 1) * qk]
        wqa.append(jnp.concatenate([wh, zq], axis=1))
        wqb.append(jnp.concatenate([zn, _rope_partner(wh[:, MLA_NOPE:]), zq], axis=1))
    eye = jnp.eye(MLA_ROPE, dtype=F32)
    wka, wkb, wv = [], [], []
    for hd in range(MLA_HEADS):
        wh = mla_w_ukv[l][:, hd * (MLA_NOPE + MLA_V):(hd + 1) * (MLA_NOPE + MLA_V)]
        top = jnp.concatenate([wh[:, 0:MLA_NOPE], jnp.zeros((MLA_KV_RANK, HEAD_PAD - MLA_NOPE), F32)], axis=1)
        mid_a = jnp.concatenate([jnp.zeros((MLA_ROPE, MLA_NOPE), F32), eye,
                                 jnp.zeros((MLA_ROPE, HEAD_PAD - qk), F32)], axis=1)
        mid_b = jnp.concatenate([jnp.zeros((MLA_ROPE, MLA_NOPE), F32), _rope_partner(eye),
                                 jnp.zeros((MLA_ROPE, HEAD_PAD - qk), F32)], axis=1)
        bot = jnp.zeros((256 - MLA_KV_RANK - MLA_ROPE, HEAD_PAD), F32)
        wka.append(jnp.concatenate([top, mid_a, bot], axis=0))
        wkb.append(jnp.concatenate([jnp.zeros_like(top), mid_b, bot], axis=0))
        wv.append(jnp.concatenate([wh[:, MLA_NOPE:], jnp.zeros((256 - MLA_KV_RANK, MLA_V), F32)], axis=0))

    ch = np.arange(FNET_GROUP_W)
    ang = 2.0 * np.pi * ((ch[:, None] * ch[None, :]) % FNET_GROUP_W) / FNET_GROUP_W
    wc = np.concatenate([np.cos(ang), np.sin(ang)], axis=1) / math.sqrt(FNET_GROUP_W)

    wg = jnp.zeros((128, 2 * GLA_QK_W), F32)
    wg = wg.at[0:GLA_GATE_RANK, 0:GLA_QK_W].set(gla_w_gate_f[l])
    wg = wg.at[GLA_GATE_RANK:2 * GLA_GATE_RANK, GLA_QK_W:].set(gla_w_gate_b[l])

    return {
        "norm1_g": norm1_g[l].reshape(1, d),
        "w_in": w_in_p,
        "q_norm_g": mla_q_norm_g[l].reshape(1, -1),
        "wqa": jnp.concatenate(wqa, axis=1).astype(BF16),
        "wqb": jnp.concatenate(wqb, axis=1).astype(BF16),
        "kv_norm_g": mla_kv_norm_g[l].reshape(1, -1),
        "wka": jnp.concatenate(wka, axis=1).astype(BF16),
        "wkb": jnp.concatenate(wkb, axis=1).astype(BF16),
        "wv": jnp.concatenate(wv, axis=1).astype(BF16),
        "wc": jnp.asarray(wc, BF16),
        "wg": wg.astype(BF16),
        "bg": jnp.concatenate([gla_b_gate_f[l], gla_b_gate_b[l]]).reshape(1, -1),
        "gla_norm_g": gla_norm_g[l].reshape(1, -1),
        "w_br_mla": w_br_mla[l].astype(BF16),
        "w_br_fnet": w_br_fnet[l].astype(BF16),
        "w_br_gla": w_br_gla[l].astype(BF16),
        "w_o": w_o[l].astype(BF16),
        "norm2_g": norm2_g[l].reshape(1, d),
        "router_w": _pad_cols(router_w[l], ROUTER_PAD).astype(BF16),
        "router_b": _pad_cols(router_b[l].reshape(1, -1), ROUTER_PAD),
        "w_up": exp_w_up[l].astype(BF16),
        "b_up": exp_b_up[l].reshape(N_EXPERTS, 1, -1),
        "w_down": exp_w_down[l].astype(BF16),
        "b_down": exp_b_down[l].reshape(N_EXPERTS, 1, -1),
    }


def _rope_tables(seq, ctx):
    rows = seq // GRID_W
    row = jnp.repeat(jnp.arange(rows, dtype=F32), GRID_W)
    col = jnp.tile(jnp.arange(GRID_W, dtype=F32), rows)
    inv_freq = ROPE_BASE ** (-jnp.arange(0, ROPE_AXIS, 2, dtype=F32) / ROPE_AXIS)
    ang_r = row[:, None] * inv_freq
    ang_c = col[:, None] * inv_freq
    ang = jnp.concatenate([ang_r, ang_r, ang_c, ang_c], axis=1)
    ang = jnp.concatenate([ang, jnp.zeros((ctx, MLA_ROPE), F32)], axis=0)
    t_all = seq + ctx
    pad = jnp.zeros((t_all, HEAD_PAD - MLA_NOPE - MLA_ROPE), F32)
    cos_t = jnp.concatenate([jnp.ones((t_all, MLA_NOPE), F32), jnp.cos(ang), pad], axis=1)
    sin_t = jnp.concatenate([jnp.zeros((t_all, MLA_NOPE), F32), jnp.sin(ang), pad], axis=1)
    return cos_t, sin_t


def _dft_mats(t):
    r = np.arange(t, dtype=np.int64)
    ang = 2.0 * np.pi * ((r[:, None] * r[None, :]) % t) / t
    s = 1.0 / math.sqrt(t)
    return jnp.asarray(np.cos(ang) * s, BF16), jnp.asarray(-np.sin(ang) * s, BF16)


def _routing(top_idx, n_tok):
    n_assign = n_tok * TOP_K
    flat_e = top_idx.reshape(n_assign)
    rt = 1024
    assert n_assign % rt == 0
    onehot = (flat_e[:, None] == jnp.arange(N_EXPERTS, dtype=I32)[None, :])
    oh3 = onehot.reshape(n_assign // rt, rt, N_EXPERTS).astype(BF16)
    tri = jnp.asarray(np.tril(np.ones((rt, rt), np.float32)), BF16)
    within = jnp.einsum("ts,nse->nte", tri, oh3, preferred_element_type=F32)
    tile_cnt = within[:, -1, :]
    tile_off = jnp.cumsum(tile_cnt, axis=0) - tile_cnt
    csum = (within + tile_off[:, None, :]).reshape(n_assign, N_EXPERTS)
    counts = (tile_off[-1] + tile_cnt[-1]).astype(I32)
    rank = jnp.sum(jnp.where(onehot, csum, 0.0), axis=1).astype(I32) - 1
    padded = ((counts + MOE_BLOCK - 1) // MOE_BLOCK) * MOE_BLOCK
    pad_end = jnp.cumsum(padded)
    pad_start = pad_end - padded
    dest = (pad_start[flat_e] + rank).astype(I32)
    n_blocks = -(-n_assign // MOE_BLOCK) + N_EXPERTS
    n_rows = n_blocks * MOE_BLOCK
    row_tok = jnp.zeros((n_rows,), I32).at[dest].set(jnp.arange(n_assign, dtype=I32) // TOP_K,
                                                     unique_indices=True)
    blk_start = jnp.arange(n_blocks, dtype=I32) * MOE_BLOCK
    blk_e = jnp.minimum(jnp.sum((pad_end[None, :] <= blk_start[:, None]).astype(I32), axis=1), N_EXPERTS - 1)
    return blk_e, row_tok.reshape(n_blocks, MOE_BLOCK), dest


def kernel(x, c, ctx, c_ctx, w_mod, b_mod, norm1_g, w_in, mla_q_norm_g, mla_w_uq, mla_kv_norm_g, mla_w_ukv,
           gla_w_gate_f, gla_b_gate_f, gla_w_gate_b, gla_b_gate_b, gla_norm_g, w_br_mla, w_br_fnet, w_br_gla,
           w_o, norm2_g, router_w, router_b, exp_w_up, exp_b_up, exp_w_down, exp_b_down, final_norm_g):
    b, seq, d = x.shape
    n_ctx = ctx.shape[1]
    n_layers = w_mod.shape[0]
    t_all = seq + n_ctx
    assert d == D_MODEL and seq % n_ctx == 0 and seq % GRID_W == 0 and n_ctx % GLA_CHUNK == 0
    tm = min(256, n_ctx)
    assert n_ctx % tm == 0 and seq % tm == 0 and (tm * TOP_K) % MOE_BLOCK == 0
    n_lat_tiles = seq // tm
    t_fourier = min(512, seq)
    t_attn = min(512, seq)

    xa = jnp.concatenate([x, ctx], axis=1)

    rows = -(-(b + 1) // 8) * 8
    cv = jnp.zeros((rows, d), F32).at[0:b].set(c).at[b].set(c_ctx)
    mod_all = _mod_all(cv, w_mod, b_mod)
    cos_t, sin_t = _rope_tables(seq, n_ctx)
    cm_l, sm_l = _dft_mats(seq)
    cm_c, sm_c = _dft_mats(n_ctx)

    n_tok = b * t_all
    for l in range(n_layers):
        lw = _layer_weights(l, w_in, mla_q_norm_g, mla_w_uq, mla_kv_norm_g, mla_w_ukv, gla_w_gate_f,
                            gla_b_gate_f, gla_w_gate_b, gla_b_gate_b, gla_norm_g, w_br_mla, w_br_fnet,
                            w_br_gla, w_o, norm1_g, norm2_g, router_w, router_b, exp_w_up, exp_b_up,
                            exp_w_down, exp_b_down)
        mod_lat = mod_all[l, 0:b]
        mod_ctx = jnp.broadcast_to(mod_all[l, b][None, :], (b, 6 * d))
        mod = jnp.stack([mod_lat, mod_ctx], axis=1).reshape(b, 2, 1, 6 * d)

        qc, kc, v, pq, gq, gk, gv, lfb, og, gate = _inproj(xa, mod, cos_t, sin_t, lw, tm, n_lat_tiles)
        o_mla_lat = _attention(qc, kc, v, t_attn, 0, seq, 0, t_all)
        o_mla_ctx = _attention(qc, kc, v, tm, seq, n_ctx, seq, n_ctx)
        y_fn_lat = _fourier(pq, cm_l, sm_l, t_fourier, 0)
        y_fn_ctx = _fourier(pq, cm_c, sm_c, n_ctx, seq)
        o_f, o_b = _gla(gq, gk, gv, lfb, tm, seq, n_ctx)
        xa, h2, top_idx, gates = _merge(xa, mod, o_mla_lat, o_mla_ctx, y_fn_lat, y_fn_ctx, o_f, o_b, og, gate,
                                        lw, tm, n_lat_tiles)

        blk_e, row_tok, dest = _routing(top_idx[..., 0:TOP_K], n_tok)
        y_sorted = _moe_experts(blk_e, row_tok, h2.reshape(n_tok, ROW_TILES, 128), lw)
        pos = dest.reshape(n_tok // tm, tm, TOP_K).transpose(0, 2, 1).reshape(n_tok // tm, tm * TOP_K)
        xa = _combine(xa, mod, gates, pos, y_sorted, tm, n_lat_tiles)

    return _final_norm(xa, final_norm_g, seq, tm)
```

```python
import functools
import math

import jax
import jax.numpy as jnp
import numpy as np
from jax import lax
from jax.experimental import pallas as pl
from jax.experimental.pallas import tpu as pltpu

F32 = jnp.float32
BF16 = jnp.bfloat16
I32 = jnp.int32
U32 = jnp.uint32

D_MODEL = 1024
GRID_W = 64
EPS = 1e-6

MLA_HEADS = 8
MLA_Q_RANK = 256
MLA_KV_RANK = 128
MLA_NOPE = 64
MLA_ROPE = 32
MLA_V = 64
MLA_SCALE = (MLA_NOPE + MLA_ROPE) ** -0.5
ROPE_AXIS = MLA_ROPE // 2
ROPE_BASE = 10000.0
HEAD_PAD = 128

FNET_GROUPS = 4
FNET_GROUP_W = 128
FNET_W = FNET_GROUPS * FNET_GROUP_W

GLA_HEADS = 4
GLA_DK = 64
GLA_DV = 128
GLA_GATE_RANK = 16
GLA_TAU = 16.0
GLA_CHUNK = 64
GLA_SUB = 16
GLA_SCALE = GLA_DK ** -0.5
GLA_QK_W = GLA_HEADS * GLA_DK
GLA_W = GLA_HEADS * GLA_DV

N_EXPERTS = 32
TOP_K = 4
D_EXPERT = D_MODEL
SWIGLU_ALPHA = 1.702
SWIGLU_LIMIT = 7.0
MOE_BLOCK = 256
ROUTER_PAD = 128

MLA_W = MLA_HEADS * MLA_V

VMEM_LIMIT_V7X = 56 * 1024 * 1024

C_Q = 0
C_KV = C_Q + MLA_Q_RANK
C_FN = C_KV + 256
C_GQ = C_FN + FNET_W
C_GK = C_GQ + GLA_QK_W
C_GV = C_GK + GLA_QK_W
C_OG = C_GV + GLA_W
C_GG = C_OG + GLA_W
C_GATE = C_GG + 128
IN_WIDTH_PAD = C_GATE + 3 * D_MODEL


def _cparams(sem, vmem=VMEM_LIMIT_V7X):
    return pltpu.CompilerParams(dimension_semantics=sem, vmem_limit_bytes=vmem)


def _rms(x, g):
    return x * lax.rsqrt(jnp.mean(x * x, axis=-1, keepdims=True) + EPS) * g


def _dot(a, b):
    return jnp.dot(a, b, preferred_element_type=F32)


def _dot_nt(a, b):
    return lax.dot_general(a, b, (((1,), (1,)), ((), ())), preferred_element_type=F32)


def _dot_tn(a, b):
    return lax.dot_general(a, b, (((0,), (0,)), ((), ())), preferred_element_type=F32)


def _pack_bf16_pairs(x):
    w = x.shape[1] // 2
    r = lax.bitcast_convert_type(x.astype(BF16).astype(F32), U32)
    return r[:, w:] | (r[:, :w] >> 16)


def _unpack_bf16_pairs(u):
    lo = lax.bitcast_convert_type(u << 16, F32)
    hi = lax.bitcast_convert_type(u & jnp.uint32(0xFFFF0000), F32)
    return lo, hi


ROW_TILES = D_MODEL // 2 // 128


def _store_row_tiles(ref, x):
    for j in range(ROW_TILES):
        ref[:, j, :] = x[:, j * 128:(j + 1) * 128]


def _load_row_tiles(ref):
    return jnp.concatenate([ref[:, j, :] for j in range(ROW_TILES)], axis=1)


def _mod_kernel(cv_ref, w_ref, b_ref, o_ref):
    cv = cv_ref[...]
    s = cv * jax.nn.sigmoid(cv)
    o_ref[...] = jnp.dot(s, w_ref[...], preferred_element_type=F32,
                         precision=lax.Precision.HIGHEST) + b_ref[...]


def _mod_all(cv, w_mod, b_mod):
    n_layers, d, w6 = w_mod.shape
    rows = cv.shape[0]
    tn = 1536
    return pl.pallas_call(
        _mod_kernel,
        out_shape=jax.ShapeDtypeStruct((n_layers, rows, w6), F32),
        grid=(n_layers, w6 // tn),
        in_specs=[
            pl.BlockSpec((rows, d), lambda l, j: (0, 0)),
            pl.BlockSpec((None, d, tn), lambda l, j: (l, 0, j)),
            pl.BlockSpec((None, 1, tn), lambda l, j: (l, 0, j)),
        ],
        out_specs=pl.BlockSpec((None, rows, tn), lambda l, j: (l, 0, j)),
        compiler_params=_cparams(("parallel", "parallel")),
        name="adaln_mod",
    )(cv, w_mod, b_mod.reshape(n_layers, 1, w6))


def _inproj_kernel(x_ref, mod_ref, cos_ref, sin_ref, g1_ref, win_ref, qng_ref, wqa_ref, wqb_ref,
                   kvg_ref, wka_ref, wkb_ref, wv_ref, wc_ref, wg_ref, bg_ref,
                   q_out, k_out, v_out, pq_out, gq_out, gk_out, gv_out, lfb_out, og_out, gate_out):
    d = D_MODEL
    x = x_ref[...]
    m = mod_ref[...]
    h = (_rms(x, g1_ref[...]) * (1.0 + m[:, d:2 * d]) + m[:, 0:d]).astype(BF16)

    def proj(a, b):
        return _dot(h, win_ref[:, a:b])

    cos = jnp.tile(cos_ref[...], (1, MLA_HEADS))
    sin = jnp.tile(sin_ref[...], (1, MLA_HEADS))

    nq = _rms(proj(C_Q, C_KV), qng_ref[...]).astype(BF16)
    q = (_dot(nq, wqa_ref[...]) * cos + _dot(nq, wqb_ref[...]) * sin) * MLA_SCALE
    q_out[...] = q.astype(BF16)
    ukv = proj(C_KV, C_FN)
    ckn = _rms(ukv[:, 0:MLA_KV_RANK], kvg_ref[...])
    lhs = jnp.concatenate([ckn, ukv[:, MLA_KV_RANK:]], axis=1).astype(BF16)
    k_out[...] = (_dot(lhs, wka_ref[...]) * cos + _dot(lhs, wkb_ref[...]) * sin).astype(BF16)
    v_out[...] = _dot(lhs, wv_ref[...]).astype(BF16)

    ufn = proj(C_FN, C_GQ).astype(BF16)
    ps, qs = [], []
    for g in range(FNET_GROUPS):
        r = _dot(ufn[:, g * FNET_GROUP_W:(g + 1) * FNET_GROUP_W], wc_ref[...])
        ps.append(r[:, 0:FNET_GROUP_W])
        qs.append(r[:, FNET_GROUP_W:])
    pq_out[...] = jnp.concatenate(ps + qs, axis=1).astype(BF16)

    gq_out[...] = (proj(C_GQ, C_GK) * GLA_SCALE).astype(BF16)
    gk_out[...] = proj(C_GK, C_GV).astype(BF16)
    gv_out[...] = proj(C_GV, C_OG).astype(BF16)
    og_out[...] = proj(C_OG, C_GG).astype(BF16)
    z = _dot(proj(C_GG, C_GATE).astype(BF16), wg_ref[...]) + bg_ref[...]
    lfb_out[...] = (jnp.minimum(z, 0.0) - jnp.log1p(jnp.exp(-jnp.abs(z)))) * (1.0 / GLA_TAU)
    gate_out[...] = proj(C_GATE, IN_WIDTH_PAD).astype(BF16)


def _inproj(xa, mod, cos_t, sin_t, lw, tm, n_lat_tiles):
    b, t_all, d = xa.shape
    nt = t_all // tm

    def tok(w, dt):
        return jax.ShapeDtypeStruct((b, t_all, w), dt)

    def tspec(w):
        return pl.BlockSpec((None, tm, w), lambda bi, i: (bi, i, 0))

    def cspec(a):
        return pl.BlockSpec(a.shape, lambda bi, i: (0,) * a.ndim)

    consts = [lw["norm1_g"], lw["w_in"], lw["q_norm_g"], lw["wqa"], lw["wqb"], lw["kv_norm_g"],
              lw["wka"], lw["wkb"], lw["wv"], lw["wc"], lw["wg"], lw["bg"]]
    widths = [(1024, BF16), (1024, BF16), (MLA_W, BF16), (2 * FNET_W, BF16), (GLA_QK_W, BF16),
              (GLA_QK_W, BF16), (GLA_W, BF16), (2 * GLA_QK_W, F32), (GLA_W, BF16), (3 * D_MODEL, BF16)]
    return pl.pallas_call(
        _inproj_kernel,
        out_shape=[tok(w, dt) for w, dt in widths],
        grid=(b, nt),
        in_specs=[
            tspec(d),
            pl.BlockSpec((None, None, 1, 6 * d), lambda bi, i: (bi, jnp.where(i < n_lat_tiles, 0, 1), 0, 0)),
            pl.BlockSpec((tm, HEAD_PAD), lambda bi, i: (i, 0)),
            pl.BlockSpec((tm, HEAD_PAD), lambda bi, i: (i, 0)),
        ] + [cspec(a) for a in consts],
        out_specs=[tspec(w) for w, _ in widths],
        compiler_params=_cparams(("parallel", "parallel")),
        name="inproj",
    )(xa, mod, cos_t, sin_t, *consts)


def _attn_kernel(q_ref, k_ref, v_ref, o_ref):
    lane = lax.broadcasted_iota(I32, (q_ref.shape[0], 2 * MLA_V), 1)
    outs = []
    for hp in range(MLA_HEADS // 2):
        vp = v_ref[:, hp * 2 * MLA_V:(hp + 1) * 2 * MLA_V]
        rs = []
        for j in range(2):
            hd = 2 * hp + j
            q = q_ref[:, hd * HEAD_PAD:(hd + 1) * HEAD_PAD]
            k = k_ref[:, hd * HEAD_PAD:(hd + 1) * HEAD_PAD]
            s = _dot_nt(q, k)
            p = jnp.exp(s - jnp.max(s, axis=-1, keepdims=True))
            l = jnp.sum(p, axis=-1, keepdims=True)
            rs.append(_dot(p.astype(BF16), vp) / l)
        outs.append(jnp.where(lane < MLA_V, rs[0], rs[1]))
    o_ref[...] = jnp.concatenate(outs, axis=1).astype(BF16)


def _attention(qc, kc, v, tq, q_row0, n_q, k_row0, n_k):
    b = qc.shape[0]
    assert q_row0 % tq == 0 and n_q % tq == 0 and k_row0 % n_k == 0
    return pl.pallas_call(
        _attn_kernel,
        out_shape=jax.ShapeDtypeStruct((b, n_q, MLA_W), BF16),
        grid=(b, n_q // tq),
        in_specs=[
            pl.BlockSpec((None, tq, MLA_HEADS * HEAD_PAD), lambda bi, i: (bi, q_row0 // tq + i, 0)),
            pl.BlockSpec((None, n_k, MLA_HEADS * HEAD_PAD), lambda bi, i: (bi, k_row0 // n_k, 0),
                         pipeline_mode=pl.Buffered(1)),
            pl.BlockSpec((None, n_k, MLA_W), lambda bi, i: (bi, k_row0 // n_k, 0),
                         pipeline_mode=pl.Buffered(1)),
        ],
        out_specs=pl.BlockSpec((None, tq, MLA_W), lambda bi, i: (bi, i, 0)),
        compiler_params=_cparams(("parallel", "parallel")),
        name="mla_attention",
    )(qc, kc, v)


def _fourier_kernel(cm_ref, sm_ref, pq_ref, o_ref):
    y = _dot(cm_ref[...], pq_ref[:, 0:FNET_W]) + _dot(sm_ref[...], pq_ref[:, FNET_W:])
    o_ref[...] = y.astype(BF16)


def _fourier(pq, cm, sm, tm, row0):
    b = pq.shape[0]
    t = cm.shape[0]
    return pl.pallas_call(
        _fourier_kernel,
        out_shape=jax.ShapeDtypeStruct((b, t, FNET_W), BF16),
        grid=(t // tm, b),
        in_specs=[
            pl.BlockSpec((tm, t), lambda i, bi: (i, 0)),
            pl.BlockSpec((tm, t), lambda i, bi: (i, 0)),
            pl.BlockSpec((None, t, 2 * FNET_W), lambda i, bi: (bi, row0 // t, 0)),
        ],
        out_specs=pl.BlockSpec((None, tm, FNET_W), lambda i, bi: (bi, i, 0)),
        compiler_params=_cparams(("parallel", "parallel")),
        name="fourier_mix",
    )(cm, sm, pq)


def _gla_chunk(q, k, v, f, tri, ind_ref, dmask_ref, st_ref, reverse):
    c = GLA_CHUNK
    f0 = f.astype(BF16)
    r1 = f - f0.astype(F32)
    f1 = r1.astype(BF16)
    f2 = (r1 - f1.astype(F32)).astype(BF16)
    g = _dot(tri, f0) + _dot(tri, f1) + _dot(tri, f2)
    e = g[0:1, :] if reverse else g[c - 1:c, :]
    qf = q.astype(F32)
    kf = k.astype(F32)
    qg = (qf * jnp.exp(g)).astype(BF16)
    kd = (kf * jnp.exp(e - g)).astype(BF16)
    st = st_ref[...]
    st_b = st.astype(BF16)

    sb = GLA_SUB
    ind = ind_ref[...]
    row_sb = lax.broadcasted_iota(I32, (sb, GLA_QK_W), 0)
    row_c = lax.broadcasted_iota(I32, (c, GLA_QK_W), 0)
    a_rows = []
    for i in range(c // sb):
        lo = i * sb
        gb, kb, qb = g[lo:lo + sb], kf[lo:lo + sb], qf[lo:lo + sb]
        ws = []
        for t in range(sb):
            w = jnp.exp(jnp.minimum(gb[t:t + 1] - gb, 0.0)) * kb * qb[t:t + 1]
            keep = (row_sb >= t) if reverse else (row_sb <= t)
            ws.append(jnp.where(keep, w, 0.0))
        w_all = jnp.concatenate(ws, axis=0).astype(BF16)
        r = _dot(w_all, ind) * dmask_ref[i]
        a_i = jnp.sum(r.reshape(sb, sb, GLA_QK_W), axis=1)
        ref_row = lo + sb if reverse else lo - 1
        if 0 <= ref_row < c:
            gr = g[ref_row:ref_row + 1]
            qt = (qb * jnp.exp(gb - gr)).astype(BF16)
            valid = (row_c >= lo + sb) if reverse else (row_c < lo)
            kt = jnp.where(valid, kf * jnp.exp(jnp.minimum(gr - g, 0.0)), 0.0).astype(BF16)
            kbd = jnp.tile(kt, (GLA_HEADS, 1)) * ind
            a_i = a_i + _dot_nt(qt, kbd)
        a_rows.append(a_i)
    a = jnp.concatenate(a_rows, axis=0).astype(BF16)

    outs, upd = [], []
    for hd in range(GLA_HEADS):
        ks_ = slice(hd * GLA_DK, (hd + 1) * GLA_DK)
        vh = v[:, hd * GLA_DV:(hd + 1) * GLA_DV]
        o_inter = _dot_nt(qg[:, ks_], st_b[:, ks_])
        o_intra = _dot(a[:, ks_], vh)
        outs.append(o_inter + o_intra)
        upd.append(_dot_tn(vh, kd[:, ks_]))
    st_ref[...] = st * jnp.exp(e) + jnp.concatenate(upd, axis=1)
    return jnp.concatenate(outs, axis=1)


def _gla_kernel(qf_ref, kf_ref, vf_ref, ff_ref, qb_ref, kb_ref, vb_ref, fb_ref,
                tril_ref, triu_ref, ind_ref, dmask_ref, of_ref, ob_ref,
                stf, stb, *, n_chunks):
    @pl.when(pl.program_id(1) == 0)
    def _():
        stf[...] = jnp.zeros_like(stf)
        stb[...] = jnp.zeros_like(stb)

    c = GLA_CHUNK
    for j in range(n_chunks):
        rows = slice(j * c, (j + 1) * c)
        of_ref[rows, :] = _gla_chunk(qf_ref[rows, :], kf_ref[rows, :], vf_ref[rows, :], ff_ref[rows, :],
                                     tril_ref[...], ind_ref, dmask_ref, stf, False)
    for j in reversed(range(n_chunks)):
        rows = slice(j * c, (j + 1) * c)
        ob_ref[rows, :] = _gla_chunk(qb_ref[rows, :], kb_ref[rows, :], vb_ref[rows, :], fb_ref[rows, :],
                                     triu_ref[...], ind_ref, dmask_ref, stb, True)


def _gla(gq, gk, gv, lfb, tb, seq, ctx):
    b, t_all, _ = gq.shape
    nlb, ncb = seq // tb, ctx // tb
    c = GLA_CHUNK

    def fwd(i):
        return jnp.where(i < ncb, nlb + i, i - ncb)

    def bwd(i):
        return jnp.where(i < ncb, nlb + ncb - 1 - i, nlb - 1 - (i - ncb))

    def spec(w, order, col=0):
        return pl.BlockSpec((None, tb, w), lambda bi, i: (bi, order(i), col))

    r = np.arange(c)
    tril = jnp.asarray(r[:, None] >= r[None, :], BF16)
    triu = jnp.asarray(r[:, None] <= r[None, :], BF16)
    hs = np.arange(GLA_QK_W)
    ind = jnp.asarray(hs[:, None] // GLA_DK == hs[None, :] // GLA_DK, BF16)
    sb = GLA_SUB
    s_loc = np.tile(np.arange(sb), sb)
    dmask = jnp.asarray(np.stack([(i * sb + s_loc)[:, None] == (hs % GLA_DK)[None, :] for i in range(c // sb)]),
                        F32)

    def cspec(a):
        return pl.BlockSpec(a.shape, lambda bi, i: (0,) * a.ndim)

    return pl.pallas_call(
        functools.partial(_gla_kernel, n_chunks=tb // c),
        out_shape=[jax.ShapeDtypeStruct((b, t_all, GLA_W), F32)] * 2,
        grid=(b, nlb + ncb),
        in_specs=[spec(GLA_QK_W, fwd), spec(GLA_QK_W, fwd), spec(GLA_W, fwd), spec(GLA_QK_W, fwd, 0),
                  spec(GLA_QK_W, bwd), spec(GLA_QK_W, bwd), spec(GLA_W, bwd), spec(GLA_QK_W, bwd, 1),
                  cspec(tril), cspec(triu), cspec(ind), cspec(dmask)],
        out_specs=[spec(GLA_W, fwd), spec(GLA_W, bwd)],
        scratch_shapes=[pltpu.VMEM((GLA_DV, GLA_QK_W), F32), pltpu.VMEM((GLA_DV, GLA_QK_W), F32)],
        compiler_params=_cparams(("parallel", "arbitrary")),
        name="gla_scan",
    )(gq, gk, gv, lfb, gq, gk, gv, lfb, tril, triu, ind, dmask)


def _merge_kernel(x_ref, mod_ref, oml_ref, omc_ref, yfl_ref, yfc_ref, of_ref, ob_ref, og_ref, gate_ref,
                  gng_ref, wbm_ref, wbf_ref, wbg_ref, wo_ref, g2_ref, rw_ref, rb_ref,
                  x_out, h_out, idx_out, gw_out, *, n_lat_tiles):
    d = D_MODEL
    m = mod_ref[...]
    is_lat = pl.program_id(1) < n_lat_tiles
    y_fn = jnp.where(is_lat, yfl_ref[...], yfc_ref[...])
    o_mla = jnp.where(is_lat, oml_ref[...], omc_ref[...])
    o = of_ref[...] + ob_ref[...]
    gng = gng_ref[...]
    og = og_ref[...].astype(F32)
    parts = []
    for hd in range(GLA_HEADS):
        sl = slice(hd * GLA_DV, (hd + 1) * GLA_DV)
        parts.append(_rms(o[:, sl], gng[:, sl]))
    o_gla = (jnp.concatenate(parts, axis=1) * (og * jax.nn.sigmoid(og))).astype(BF16)

    gate = jax.nn.sigmoid(gate_ref[...].astype(F32))
    y = (gate[:, 0:d] * _dot(o_mla, wbm_ref[...])
         + gate[:, d:2 * d] * _dot(y_fn, wbf_ref[...])
         + gate[:, 2 * d:3 * d] * _dot(o_gla, wbg_ref[...]))
    x1 = x_ref[...] + m[:, 2 * d:3 * d] * _dot(y.astype(BF16), wo_ref[...])
    x_out[...] = x1
    h2 = _rms(x1, g2_ref[...]) * (1.0 + m[:, 4 * d:5 * d]) + m[:, 3 * d:4 * d]
    _store_row_tiles(h_out, _pack_bf16_pairs(h2))

    logits = _dot(h2.astype(BF16), rw_ref[...]) + rb_ref[...]
    lane = lax.broadcasted_iota(I32, logits.shape, 1)
    lane_f = lane.astype(F32)
    neg = jnp.float32(-jnp.inf)
    lg = jnp.where(lane < N_EXPERTS, logits, neg)
    idx_acc = jnp.zeros(logits.shape, I32)
    val_acc = jnp.zeros(logits.shape, F32)
    v0 = None
    for kk in range(TOP_K):
        mx = jnp.max(lg, axis=-1, keepdims=True)
        ix = jnp.min(jnp.where(lg == mx, lane_f, float(ROUTER_PAD)), axis=-1, keepdims=True).astype(I32)
        if kk == 0:
            v0 = mx
        idx_acc = jnp.where(lane == kk, ix, idx_acc)
        val_acc = jnp.where(lane == kk, jnp.exp(mx - v0), val_acc)
        lg = jnp.where(lane == ix, neg, lg)
    idx_out[...] = idx_acc
    gw_out[...] = val_acc / jnp.sum(val_acc, axis=-1, keepdims=True)


def _merge(xa, mod, o_mla_lat, o_mla_ctx, y_fn_lat, y_fn_ctx, o_f, o_b, og, gate, lw, tm, n_lat_tiles):
    b, t_all, d = xa.shape

    def tspec(w):
        return pl.BlockSpec((None, tm, w), lambda bi, i: (bi, i, 0))

    def lat_spec(w):
        return pl.BlockSpec((None, tm, w), lambda bi, i: (bi, jnp.minimum(i, n_lat_tiles - 1), 0))

    def ctx_spec(w):
        return pl.BlockSpec((None, tm, w), lambda bi, i: (bi, jnp.maximum(i - n_lat_tiles, 0), 0))

    def cspec(a):
        return pl.BlockSpec(a.shape, lambda bi, i: (0,) * a.ndim)

    consts = [lw["gla_norm_g"], lw["w_br_mla"], lw["w_br_fnet"], lw["w_br_gla"], lw["w_o"],
              lw["norm2_g"], lw["router_w"], lw["router_b"]]
    return pl.pallas_call(
        functools.partial(_merge_kernel, n_lat_tiles=n_lat_tiles),
        out_shape=[jax.ShapeDtypeStruct((b, t_all, d), F32), jax.ShapeDtypeStruct((b, t_all, ROW_TILES, 128), U32),
                   jax.ShapeDtypeStruct((b, t_all, ROUTER_PAD), I32),
                   jax.ShapeDtypeStruct((b, t_all, ROUTER_PAD), F32)],
        grid=(b, t_all // tm),
        in_specs=[
            tspec(d),
            pl.BlockSpec((None, None, 1, 6 * d), lambda bi, i: (bi, jnp.where(i < n_lat_tiles, 0, 1), 0, 0)),
            lat_spec(MLA_W), ctx_spec(MLA_W), lat_spec(FNET_W), ctx_spec(FNET_W),
            tspec(GLA_W), tspec(GLA_W), tspec(GLA_W), tspec(3 * d),
        ] + [cspec(a) for a in consts],
        out_specs=[tspec(d), pl.BlockSpec((None, tm, ROW_TILES, 128), lambda bi, i: (bi, i, 0, 0)),
                   tspec(ROUTER_PAD), tspec(ROUTER_PAD)],
        input_output_aliases={0: 0},
        compiler_params=_cparams(("parallel", "parallel")),
        name="merge_router",
    )(xa, mod, o_mla_lat, o_mla_ctx, y_fn_lat, y_fn_ctx, o_f, o_b, og, gate, *consts)


def _row_gather(idx_smem, slot, src_hbm, dst, sem, n_rows):
    for r in range(n_rows):
        pltpu.make_async_copy(src_hbm.at[pl.ds(idx_smem[slot, r], 1)], dst.at[slot, pl.ds(r, 1)],
                              sem.at[slot]).start()


def _gather_step(i, n, idx_hbm, src_hbm, idx_smem, buf, isem, gsem, n_rows):
    slot = lax.rem(i, 2)
    nxt = 1 - slot
    last = n - 1

    def idx_copy(blk, s):
        return pltpu.make_async_copy(idx_hbm.at[blk], idx_smem.at[s], isem.at[s])

    def rows_wait(s):
        pltpu.make_async_copy(buf.at[s], buf.at[s], gsem.at[s]).wait()

    @pl.when(i == 0)
    def _():
        idx_copy(0, 0).start()
        idx_copy(0, 0).wait()
        _row_gather(idx_smem, 0, src_hbm, buf, gsem, n_rows)
        idx_copy(jnp.minimum(1, last), 1).start()

    idx_copy(jnp.minimum(i + 1, last), nxt).wait()
    rows_wait(slot)
    _row_gather(idx_smem, nxt, src_hbm, buf, gsem, n_rows)
    idx_copy(jnp.minimum(i + 2, last), slot).start()

    def drain():
        @pl.when(i == last)
        def _():
            rows_wait(nxt)
            idx_copy(last, slot).wait()

    return slot, drain


def _moe_kernel(blk_e_ref, tok_hbm, h_hbm, wu_ref, bu_ref, wd_ref, bd_ref, y_out,
                idx_smem, xbuf, isem, gsem, wu_b, wd_b):
    i = pl.program_id(0)

    @pl.when((i == 0) | (blk_e_ref[i] != blk_e_ref[jnp.maximum(i - 1, 0)]))
    def _():
        wu_b[...] = wu_ref[...].astype(BF16)
        wd_b[...] = wd_ref[...].astype(BF16)

    slot, drain = _gather_step(i, pl.num_programs(0), tok_hbm, h_hbm, idx_smem, xbuf, isem, gsem, MOE_BLOCK)
    xb = jnp.concatenate(_unpack_bf16_pairs(_load_row_tiles(xbuf.at[slot])), axis=1).astype(BF16)
    up = _dot(xb, wu_b[...]) + bu_ref[...]
    glu = jnp.minimum(up[:, 0:D_EXPERT], SWIGLU_LIMIT)
    lin = jnp.clip(up[:, D_EXPERT:], -SWIGLU_LIMIT, SWIGLU_LIMIT)
    act = glu * jax.nn.sigmoid(SWIGLU_ALPHA * glu) * (lin + 1.0)
    yb = _dot(act.astype(BF16), wd_b[...]) + bd_ref[...]
    _store_row_tiles(y_out, _pack_bf16_pairs(yb))
    drain()


def _moe_experts(blk_e, row_tok, h_flat, w_up, b_up, w_down, b_down, layer):
    n_blocks = blk_e.shape[0]
    d = D_MODEL
    grid_spec = pltpu.PrefetchScalarGridSpec(
        num_scalar_prefetch=1,
        grid=(n_blocks,),
        in_specs=[
            pl.BlockSpec(memory_space=pl.ANY),
            pl.BlockSpec(memory_space=pl.ANY),
            pl.BlockSpec((None, None, d, 2 * D_EXPERT), lambda i, be: (layer, be[i], 0, 0)),
            pl.BlockSpec((None, None, 1, 2 * D_EXPERT), lambda i, be: (layer, be[i], 0, 0)),
            pl.BlockSpec((None, None, D_EXPERT, d), lambda i, be: (layer, be[i], 0, 0)),
            pl.BlockSpec((None, None, 1, d), lambda i, be: (layer, be[i], 0, 0)),
        ],
        out_specs=pl.BlockSpec((MOE_BLOCK, ROW_TILES, 128), lambda i, be: (i, 0, 0)),
        scratch_shapes=[pltpu.SMEM((2, MOE_BLOCK), I32), pltpu.VMEM((2, MOE_BLOCK, ROW_TILES, 128), U32),
                        pltpu.SemaphoreType.DMA((2,)), pltpu.SemaphoreType.DMA((2,)),
                        pltpu.VMEM((d, 2 * D_EXPERT), BF16), pltpu.VMEM((D_EXPERT, d), BF16)],
    )
    return pl.pallas_call(
        _moe_kernel,
        out_shape=jax.ShapeDtypeStruct((n_blocks * MOE_BLOCK, ROW_TILES, 128), U32),
        grid_spec=grid_spec,
        compiler_params=_cparams(("arbitrary",)),
        name="moe_experts",
    )(blk_e, row_tok, h_flat, w_up, b_up, w_down, b_down)


def _combine_kernel(pos_hbm, y_hbm, x_ref, mod_ref, gw_ref, x_out, idx_smem, ybuf, isem, gsem, *, tm, n_tiles):
    bi = pl.program_id(0)
    ti = pl.program_id(1)
    i = bi * n_tiles + ti
    n = pl.num_programs(0) * n_tiles
    slot, drain = _gather_step(i, n, pos_hbm, y_hbm, idx_smem, ybuf, isem, gsem, tm * TOP_K)
    gw = gw_ref[...]
    acc_lo = acc_hi = None
    for kk in range(TOP_K):
        lo, hi = _unpack_bf16_pairs(_load_row_tiles(ybuf.at[slot, pl.ds(kk * tm, tm)]))
        gk = gw[:, kk:kk + 1]
        acc_lo = lo * gk if kk == 0 else acc_lo + lo * gk
        acc_hi = hi * gk if kk == 0 else acc_hi + hi * gk
    d = D_MODEL
    x_out[...] = x_ref[...] + mod_ref[:, 5 * d:6 * d] * jnp.concatenate([acc_lo, acc_hi], axis=1)
    drain()


def _combine(xa, mod, gates, pos, y_sorted, tm, n_lat_tiles):
    b, t_all, d = xa.shape
    n_tiles = t_all // tm
    return pl.pallas_call(
        functools.partial(_combine_kernel, tm=tm, n_tiles=n_tiles),
        out_shape=jax.ShapeDtypeStruct((b, t_all, d), F32),
        grid=(b, n_tiles),
        in_specs=[
            pl.BlockSpec(memory_space=pl.ANY),
            pl.BlockSpec(memory_space=pl.ANY),
            pl.BlockSpec((None, tm, d), lambda bi, i: (bi, i, 0)),
            pl.BlockSpec((None, None, 1, 6 * d), lambda bi, i: (bi, jnp.where(i < n_lat_tiles, 0, 1), 0, 0)),
            pl.BlockSpec((None, tm, ROUTER_PAD), lambda bi, i: (bi, i, 0)),
        ],
        out_specs=pl.BlockSpec((None, tm, d), lambda bi, i: (bi, i, 0)),
        scratch_shapes=[pltpu.SMEM((2, tm * TOP_K), I32), pltpu.VMEM((2, tm * TOP_K, ROW_TILES, 128), U32),
                        pltpu.SemaphoreType.DMA((2,)), pltpu.SemaphoreType.DMA((2,))],
        input_output_aliases={2: 0},
        compiler_params=_cparams(("arbitrary", "arbitrary")),
        name="moe_combine",
    )(pos, y_sorted, xa, mod, gates)


def _final_kernel(x_ref, g_ref, o_ref):
    o_ref[...] = _rms(x_ref[...], g_ref[...])


def _final_norm(xa, g, seq, tm):
    b, _, d = xa.shape
    return pl.pallas_call(
        _final_kernel,
        out_shape=jax.ShapeDtypeStruct((b, seq, d), F32),
        grid=(b, seq // tm),
        in_specs=[pl.BlockSpec((None, tm, d), lambda bi, i: (bi, i, 0)),
                  pl.BlockSpec((1, d), lambda bi, i: (0, 0))],
        out_specs=pl.BlockSpec((None, tm, d), lambda bi, i: (bi, i, 0)),
        compiler_params=_cparams(("parallel", "parallel")),
        name="final_norm",
    )(xa, g.reshape(1, d))


def _pad_cols(w, width):
    return jnp.pad(w, ((0, 0), (0, width - w.shape[1])))


def _rope_partner(w):
    a = ROPE_AXIS // 2
    return jnp.concatenate([-w[:, a:2 * a], w[:, 0:a], -w[:, 3 * a:4 * a], w[:, 2 * a:3 * a]], axis=1)


def _layer_weights(l, w_in, mla_q_norm_g, mla_w_uq, mla_kv_norm_g, mla_w_ukv, gla_w_gate_f, gla_b_gate_f,
                   gla_w_gate_b, gla_b_gate_b, gla_norm_g, w_br_mla, w_br_fnet, w_br_gla, w_o, norm1_g,
                   norm2_g, router_w, router_b):
    d = D_MODEL
    splits = np.cumsum([MLA_Q_RANK, MLA_KV_RANK + MLA_ROPE, FNET_W, GLA_QK_W, GLA_QK_W, GLA_W, GLA_W,
                        GLA_GATE_RANK, GLA_GATE_RANK])
    wq, wkv, wfn, wgq, wgk, wgv, wog, wgf, wgb, wgate = jnp.split(w_in[l], [int(s) for s in splits], axis=1)
    w_in_p = jnp.concatenate([wq, _pad_cols(wkv, 256), wfn, wgq, wgk, wgv, wog,
                              _pad_cols(jnp.concatenate([wgf, wgb], axis=1), 128), wgate], axis=1).astype(BF16)

    qk = MLA_NOPE + MLA_ROPE
    zq = jnp.zeros((MLA_Q_RANK, HEAD_PAD - qk), F32)
    zn = jnp.zeros((MLA_Q_RANK, MLA_NOPE), F32)
    wqa, wqb = [], []
    for hd in range(MLA_HEADS):
        wh = mla_w_uq[l][:, hd * qk:(hd + 1) * qk]
        wqa.append(jnp.concatenate([wh, zq], axis=1))
        wqb.append(jnp.concatenate([zn, _rope_partner(wh[:, MLA_NOPE:]), zq], axis=1))
    eye = jnp.eye(MLA_ROPE, dtype=F32)
    wka, wkb, wv = [], [], []
    for hd in range(MLA_HEADS):
        wh = mla_w_ukv[l][:, hd * (MLA_NOPE + MLA_V):(hd + 1) * (MLA_NOPE + MLA_V)]
        top = jnp.concatenate([wh[:, 0:MLA_NOPE], jnp.zeros((MLA_KV_RANK, HEAD_PAD - MLA_NOPE), F32)], axis=1)
        mid_a = jnp.concatenate([jnp.zeros((MLA_ROPE, MLA_NOPE), F32), eye,
                                 jnp.zeros((MLA_ROPE, HEAD_PAD - qk), F32)], axis=1)
        mid_b = jnp.concatenate([jnp.zeros((MLA_ROPE, MLA_NOPE), F32), _rope_partner(eye),
                                 jnp.zeros((MLA_ROPE, HEAD_PAD - qk), F32)], axis=1)
        bot = jnp.zeros((256 - MLA_KV_RANK - MLA_ROPE, HEAD_PAD), F32)
        wka.append(jnp.concatenate([top, mid_a, bot], axis=0))
        wkb.append(jnp.concatenate([jnp.zeros_like(top), mid_b, bot], axis=0))
        wv.append(jnp.concatenate([wh[:, MLA_NOPE:], jnp.zeros((256 - MLA_KV_RANK, MLA_V), F32)], axis=0))

    ch = np.arange(FNET_GROUP_W)
    ang = 2.0 * np.pi * ((ch[:, None] * ch[None, :]) % FNET_GROUP_W) / FNET_GROUP_W
    wc = np.concatenate([np.cos(ang), np.sin(ang)], axis=1) / math.sqrt(FNET_GROUP_W)

    wg = jnp.zeros((128, 2 * GLA_QK_W), F32)
    wg = wg.at[0:GLA_GATE_RANK, 0:GLA_QK_W].set(gla_w_gate_f[l])
    wg = wg.at[GLA_GATE_RANK:2 * GLA_GATE_RANK, GLA_QK_W:].set(gla_w_gate_b[l])

    return {
        "norm1_g": norm1_g[l].reshape(1, d),
        "w_in": w_in_p,
        "q_norm_g": mla_q_norm_g[l].reshape(1, -1),
        "wqa": jnp.concatenate(wqa, axis=1).astype(BF16),
        "wqb": jnp.concatenate(wqb, axis=1).astype(BF16),
        "kv_norm_g": mla_kv_norm_g[l].reshape(1, -1),
        "wka": jnp.concatenate(wka, axis=1).astype(BF16),
        "wkb": jnp.concatenate(wkb, axis=1).astype(BF16),
        "wv": jnp.concatenate(wv, axis=1).astype(BF16),
        "wc": jnp.asarray(wc, BF16),
        "wg": wg.astype(BF16),
        "bg": jnp.concatenate([gla_b_gate_f[l], gla_b_gate_b[l]]).reshape(1, -1),
        "gla_norm_g": gla_norm_g[l].reshape(1, -1),
        "w_br_mla": w_br_mla[l].astype(BF16),
        "w_br_fnet": w_br_fnet[l].astype(BF16),
        "w_br_gla": w_br_gla[l].astype(BF16),
        "w_o": w_o[l].astype(BF16),
        "norm2_g": norm2_g[l].reshape(1, d),
        "router_w": _pad_cols(router_w[l], ROUTER_PAD).astype(BF16),
        "router_b": _pad_cols(router_b[l].reshape(1, -1), ROUTER_PAD),
    }


def _rope_tables(seq, ctx):
    rows = seq // GRID_W
    row = jnp.repeat(jnp.arange(rows, dtype=F32), GRID_W)
    col = jnp.tile(jnp.arange(GRID_W, dtype=F32), rows)
    inv_freq = ROPE_BASE ** (-jnp.arange(0, ROPE_AXIS, 2, dtype=F32) / ROPE_AXIS)
    ang_r = row[:, None] * inv_freq
    ang_c = col[:, None] * inv_freq
    ang = jnp.concatenate([ang_r, ang_r, ang_c, ang_c], axis=1)
    ang = jnp.concatenate([ang, jnp.zeros((ctx, MLA_ROPE), F32)], axis=0)
    t_all = seq + ctx
    pad = jnp.zeros((t_all, HEAD_PAD - MLA_NOPE - MLA_ROPE), F32)
    cos_t = jnp.concatenate([jnp.ones((t_all, MLA_NOPE), F32), jnp.cos(ang), pad], axis=1)
    sin_t = jnp.concatenate([jnp.zeros((t_all, MLA_NOPE), F32), jnp.sin(ang), pad], axis=1)
    return cos_t, sin_t


def _dft_mats(t):
    r = np.arange(t, dtype=np.int64)
    ang = 2.0 * np.pi * ((r[:, None] * r[None, :]) % t) / t
    s = 1.0 / math.sqrt(t)
    return jnp.asarray(np.cos(ang) * s, BF16), jnp.asarray(-np.sin(ang) * s, BF16)


def _routing(top_idx, n_tok):
    n_assign = n_tok * TOP_K
    flat_e = top_idx.reshape(n_assign)
    rt = 1024
    assert n_assign % rt == 0
    onehot = (flat_e[:, None] == jnp.arange(N_EXPERTS, dtype=I32)[None, :])
    oh3 = onehot.reshape(n_assign // rt, rt, N_EXPERTS).astype(BF16)
    tri = jnp.asarray(np.tril(np.ones((rt, rt), np.float32)), BF16)
    within = jnp.einsum("ts,nse->nte", tri, oh3, preferred_element_type=F32)
    tile_cnt = within[:, -1, :]
    tile_off = jnp.cumsum(tile_cnt, axis=0) - tile_cnt
    csum = (within + tile_off[:, None, :]).reshape(n_assign, N_EXPERTS)
    counts = (tile_off[-1] + tile_cnt[-1]).astype(I32)
    rank = jnp.sum(jnp.where(onehot, csum, 0.0), axis=1).astype(I32) - 1
    padded = ((counts + MOE_BLOCK - 1) // MOE_BLOCK) * MOE_BLOCK
    pad_end = jnp.cumsum(padded)
    pad_start = pad_end - padded
    dest = (pad_start[flat_e] + rank).astype(I32)
    n_blocks = -(-n_assign // MOE_BLOCK) + N_EXPERTS
    n_rows = n_blocks * MOE_BLOCK
    row_tok = jnp.zeros((n_rows,), I32).at[dest].set(jnp.arange(n_assign, dtype=I32) // TOP_K,
                                                     unique_indices=True)
    blk_start = jnp.arange(n_blocks, dtype=I32) * MOE_BLOCK
    blk_e = jnp.minimum(jnp.sum((pad_end[None, :] <= blk_start[:, None]).astype(I32), axis=1), N_EXPERTS - 1)
    return blk_e, row_tok.reshape(n_blocks, MOE_BLOCK), dest


def kernel(x, c, ctx, c_ctx, w_mod, b_mod, norm1_g, w_in, mla_q_norm_g, mla_w_uq, mla_kv_norm_g, mla_w_ukv,
           gla_w_gate_f, gla_b_gate_f, gla_w_gate_b, gla_b_gate_b, gla_norm_g, w_br_mla, w_br_fnet, w_br_gla,
           w_o, norm2_g, router_w, router_b, exp_w_up, exp_b_up, exp_w_down, exp_b_down, final_norm_g):
    b, seq, d = x.shape
    n_ctx = ctx.shape[1]
    n_layers = w_mod.shape[0]
    t_all = seq + n_ctx
    assert d == D_MODEL and seq % n_ctx == 0 and seq % GRID_W == 0 and n_ctx % GLA_CHUNK == 0
    tm = min(256, n_ctx)
    assert n_ctx % tm == 0 and seq % tm == 0 and (tm * TOP_K) % MOE_BLOCK == 0
    n_lat_tiles = seq // tm
    t_fourier = min(512, seq)
    t_attn = min(512, seq)

    xa = jnp.concatenate([x, ctx], axis=1)

    rows = -(-(b + 1) // 8) * 8
    cv = jnp.zeros((rows, d), F32).at[0:b].set(c).at[b].set(c_ctx)
    mod_all = _mod_all(cv, w_mod, b_mod)
    cos_t, sin_t = _rope_tables(seq, n_ctx)
    cm_l, sm_l = _dft_mats(seq)
    cm_c, sm_c = _dft_mats(n_ctx)

    n_tok = b * t_all
    b_up4 = exp_b_up.reshape(n_layers, N_EXPERTS, 1, 2 * D_EXPERT)
    b_down4 = exp_b_down.reshape(n_layers, N_EXPERTS, 1, d)
    for l in range(n_layers):
        lw = _layer_weights(l, w_in, mla_q_norm_g, mla_w_uq, mla_kv_norm_g, mla_w_ukv, gla_w_gate_f,
                            gla_b_gate_f, gla_w_gate_b, gla_b_gate_b, gla_norm_g, w_br_mla, w_br_fnet,
                            w_br_gla, w_o, norm1_g, norm2_g, router_w, router_b)
        mod_lat = mod_all[l, 0:b]
        mod_ctx = jnp.broadcast_to(mod_all[l, b][None, :], (b, 6 * d))
        mod = jnp.stack([mod_lat, mod_ctx], axis=1).reshape(b, 2, 1, 6 * d)

        qc, kc, v, pq, gq, gk, gv, lfb, og, gate = _inproj(xa, mod, cos_t, sin_t, lw, tm, n_lat_tiles)
        o_mla_lat = _attention(qc, kc, v, t_attn, 0, seq, 0, t_all)
        o_mla_ctx = _attention(qc, kc, v, tm, seq, n_ctx, seq, n_ctx)
        y_fn_lat = _fourier(pq, cm_l, sm_l, t_fourier, 0)
        y_fn_ctx = _fourier(pq, cm_c, sm_c, n_ctx, seq)
        o_f, o_b = _gla(gq, gk, gv, lfb, tm, seq, n_ctx)
        xa, h2, top_idx, gates = _merge(xa, mod, o_mla_lat, o_mla_ctx, y_fn_lat, y_fn_ctx, o_f, o_b, og, gate,
                                        lw, tm, n_lat_tiles)

        blk_e, row_tok, dest = _routing(top_idx[..., 0:TOP_K], n_tok)
        y_sorted = _moe_experts(blk_e, row_tok, h2.reshape(n_tok, ROW_TILES, 128), exp_w_up, b_up4, exp_w_down,
                                b_down4, l)
        pos = dest.reshape(n_tok // tm, tm, TOP_K).transpose(0, 2, 1).reshape(n_tok // tm, tm * TOP_K)
        xa = _combine(xa, mod, gates, pos, y_sorted, tm, n_lat_tiles)

    return _final_norm(xa, final_norm_g, seq, tm)
```

```python
import functools
import math

import jax
import jax.numpy as jnp
import numpy as np
from jax import lax
from jax.experimental import pallas as pl
from jax.experimental.pallas import tpu as pltpu

F32 = jnp.float32
BF16 = jnp.bfloat16
I32 = jnp.int32
U32 = jnp.uint32

D_MODEL = 1024
GRID_W = 64
EPS = 1e-6

MLA_HEADS = 8
MLA_Q_RANK = 256
MLA_KV_RANK = 128
MLA_NOPE = 64
MLA_ROPE = 32
MLA_V = 64
MLA_SCALE = (MLA_NOPE + MLA_ROPE) ** -0.5
ROPE_AXIS = MLA_ROPE // 2
ROPE_BASE = 10000.0
HEAD_PAD = 128

FNET_GROUPS = 4
FNET_GROUP_W = 128
FNET_W = FNET_GROUPS * FNET_GROUP_W

GLA_HEADS = 4
GLA_DK = 64
GLA_DV = 128
GLA_GATE_RANK = 16
GLA_TAU = 16.0
GLA_CHUNK = 64
GLA_SUB = 16
GLA_SCALE = GLA_DK ** -0.5
GLA_QK_W = GLA_HEADS * GLA_DK
GLA_W = GLA_HEADS * GLA_DV

N_EXPERTS = 32
TOP_K = 4
D_EXPERT = D_MODEL
SWIGLU_ALPHA = 1.702
SWIGLU_LIMIT = 7.0
MOE_BLOCK = 256
ROUTER_PAD = 128

MLA_W = MLA_HEADS * MLA_V

VMEM_LIMIT_V7X = 56 * 1024 * 1024

C_Q = 0
C_KV = C_Q + MLA_Q_RANK
C_FN = C_KV + 256
C_GQ = C_FN + FNET_W
C_GK = C_GQ + GLA_QK_W
C_GV = C_GK + GLA_QK_W
C_OG = C_GV + GLA_W
C_GG = C_OG + GLA_W
C_GATE = C_GG + 128
IN_WIDTH_PAD = C_GATE + 3 * D_MODEL


def _cparams(sem, vmem=VMEM_LIMIT_V7X):
    return pltpu.CompilerParams(dimension_semantics=sem, vmem_limit_bytes=vmem)


def _rms(x, g):
    return x * lax.rsqrt(jnp.mean(x * x, axis=-1, keepdims=True) + EPS) * g


def _dot(a, b):
    return jnp.dot(a, b, preferred_element_type=F32)


def _dot_nt(a, b):
    return lax.dot_general(a, b, (((1,), (1,)), ((), ())), preferred_element_type=F32)


def _dot_tn(a, b):
    return lax.dot_general(a, b, (((0,), (0,)), ((), ())), preferred_element_type=F32)


def _pack_bf16_pairs(x):
    w = x.shape[1] // 2
    r = lax.bitcast_convert_type(x.astype(BF16).astype(F32), U32)
    return r[:, w:] | (r[:, :w] >> 16)


def _unpack_bf16_pairs(u):
    lo = lax.bitcast_convert_type(u << 16, F32)
    hi = lax.bitcast_convert_type(u & jnp.uint32(0xFFFF0000), F32)
    return lo, hi


ROW_TILES = D_MODEL // 2 // 128


def _store_row_tiles(ref, x):
    for j in range(ROW_TILES):
        ref[:, j, :] = x[:, j * 128:(j + 1) * 128]


def _load_row_tiles(buf, slot, row0, m):
    return jnp.concatenate(
        [buf[slot, pl.ds(row0 * ROW_TILES + j, m, stride=ROW_TILES), :] for j in range(ROW_TILES)], axis=1)


def _mod_kernel(cv_ref, w_ref, b_ref, o_ref):
    cv = cv_ref[...]
    s = cv * jax.nn.sigmoid(cv)
    o_ref[...] = jnp.dot(s, w_ref[...], preferred_element_type=F32,
                         precision=lax.Precision.HIGHEST) + b_ref[...]


def _mod_all(cv, w_mod, b_mod):
    n_layers, d, w6 = w_mod.shape
    rows = cv.shape[0]
    tn = 1536
    return pl.pallas_call(
        _mod_kernel,
        out_shape=jax.ShapeDtypeStruct((n_layers, rows, w6), F32),
        grid=(n_layers, w6 // tn),
        in_specs=[
            pl.BlockSpec((rows, d), lambda l, j: (0, 0)),
            pl.BlockSpec((None, d, tn), lambda l, j: (l, 0, j)),
            pl.BlockSpec((None, 1, tn), lambda l, j: (l, 0, j)),
        ],
        out_specs=pl.BlockSpec((None, rows, tn), lambda l, j: (l, 0, j)),
        compiler_params=_cparams(("parallel", "parallel")),
        name="adaln_mod",
    )(cv, w_mod, b_mod.reshape(n_layers, 1, w6))


def _inproj_kernel(x_ref, mod_ref, cos_ref, sin_ref, g1_ref, win_ref, qng_ref, wqa_ref, wqb_ref,
                   kvg_ref, wka_ref, wkb_ref, wv_ref, wc_ref, wg_ref, bg_ref,
                   q_out, k_out, v_out, pq_out, gq_out, gk_out, gv_out, lfb_out, og_out, gate_out):
    d = D_MODEL
    x = x_ref[...]
    m = mod_ref[...]
    h = (_rms(x, g1_ref[...]) * (1.0 + m[:, d:2 * d]) + m[:, 0:d]).astype(BF16)

    def proj(a, b):
        return _dot(h, win_ref[:, a:b])

    cos = jnp.tile(cos_ref[...], (1, MLA_HEADS))
    sin = jnp.tile(sin_ref[...], (1, MLA_HEADS))

    nq = _rms(proj(C_Q, C_KV), qng_ref[...]).astype(BF16)
    q = (_dot(nq, wqa_ref[...]) * cos + _dot(nq, wqb_ref[...]) * sin) * MLA_SCALE
    q_out[...] = q.astype(BF16)
    ukv = proj(C_KV, C_FN)
    ckn = _rms(ukv[:, 0:MLA_KV_RANK], kvg_ref[...])
    lhs = jnp.concatenate([ckn, ukv[:, MLA_KV_RANK:]], axis=1).astype(BF16)
    k_out[...] = (_dot(lhs, wka_ref[...]) * cos + _dot(lhs, wkb_ref[...]) * sin).astype(BF16)
    v_out[...] = _dot(lhs, wv_ref[...]).astype(BF16)

    ufn = proj(C_FN, C_GQ).astype(BF16)
    ps, qs = [], []
    for g in range(FNET_GROUPS):
        r = _dot(ufn[:, g * FNET_GROUP_W:(g + 1) * FNET_GROUP_W], wc_ref[...])
        ps.append(r[:, 0:FNET_GROUP_W])
        qs.append(r[:, FNET_GROUP_W:])
    pq_out[...] = jnp.concatenate(ps + qs, axis=1).astype(BF16)

    gq_out[...] = (proj(C_GQ, C_GK) * GLA_SCALE).astype(BF16)
    gk_out[...] = proj(C_GK, C_GV).astype(BF16)
    gv_out[...] = proj(C_GV, C_OG).astype(BF16)
    og_out[...] = proj(C_OG, C_GG).astype(BF16)
    z = _dot(proj(C_GG, C_GATE).astype(BF16), wg_ref[...]) + bg_ref[...]
    lfb_out[...] = (jnp.minimum(z, 0.0) - jnp.log1p(jnp.exp(-jnp.abs(z)))) * (1.0 / GLA_TAU)
    gate_out[...] = proj(C_GATE, IN_WIDTH_PAD).astype(BF16)


def _inproj(xa, mod, cos_t, sin_t, lw, tm, n_lat_tiles):
    b, t_all, d = xa.shape
    nt = t_all // tm

    def tok(w, dt):
        return jax.ShapeDtypeStruct((b, t_all, w), dt)

    def tspec(w):
        return pl.BlockSpec((None, tm, w), lambda bi, i: (bi, i, 0))

    def cspec(a):
        return pl.BlockSpec(a.shape, lambda bi, i: (0,) * a.ndim)

    consts = [lw["norm1_g"], lw["w_in"], lw["q_norm_g"], lw["wqa"], lw["wqb"], lw["kv_norm_g"],
              lw["wka"], lw["wkb"], lw["wv"], lw["wc"], lw["wg"], lw["bg"]]
    widths = [(1024, BF16), (1024, BF16), (MLA_W, BF16), (2 * FNET_W, BF16), (GLA_QK_W, BF16),
              (GLA_QK_W, BF16), (GLA_W, BF16), (2 * GLA_QK_W, F32), (GLA_W, BF16), (3 * D_MODEL, BF16)]
    return pl.pallas_call(
        _inproj_kernel,
        out_shape=[tok(w, dt) for w, dt in widths],
        grid=(b, nt),
        in_specs=[
            tspec(d),
            pl.BlockSpec((None, None, 1, 6 * d), lambda bi, i: (bi, jnp.where(i < n_lat_tiles, 0, 1), 0, 0)),
            pl.BlockSpec((tm, HEAD_PAD), lambda bi, i: (i, 0)),
            pl.BlockSpec((tm, HEAD_PAD), lambda bi, i: (i, 0)),
        ] + [cspec(a) for a in consts],
        out_specs=[tspec(w) for w, _ in widths],
        compiler_params=_cparams(("parallel", "parallel")),
        name="inproj",
    )(xa, mod, cos_t, sin_t, *consts)


def _attn_kernel(q_ref, k_ref, v_ref, o_ref):
    lane = lax.broadcasted_iota(I32, (q_ref.shape[0], 2 * MLA_V), 1)
    outs = []
    for hp in range(MLA_HEADS // 2):
        vp = v_ref[:, hp * 2 * MLA_V:(hp + 1) * 2 * MLA_V]
        rs = []
        for j in range(2):
            hd = 2 * hp + j
            q = q_ref[:, hd * HEAD_PAD:(hd + 1) * HEAD_PAD]
            k = k_ref[:, hd * HEAD_PAD:(hd + 1) * HEAD_PAD]
            s = _dot_nt(q, k)
            p = jnp.exp(s - jnp.max(s, axis=-1, keepdims=True))
            l = jnp.sum(p, axis=-1, keepdims=True)
            rs.append(_dot(p.astype(BF16), vp) / l)
        outs.append(jnp.where(lane < MLA_V, rs[0], rs[1]))
    o_ref[...] = jnp.concatenate(outs, axis=1).astype(BF16)


def _attention(qc, kc, v, tq, q_row0, n_q, k_row0, n_k):
    b = qc.shape[0]
    assert q_row0 % tq == 0 and n_q % tq == 0 and k_row0 % n_k == 0
    return pl.pallas_call(
        _attn_kernel,
        out_shape=jax.ShapeDtypeStruct((b, n_q, MLA_W), BF16),
        grid=(b, n_q // tq),
        in_specs=[
            pl.BlockSpec((None, tq, MLA_HEADS * HEAD_PAD), lambda bi, i: (bi, q_row0 // tq + i, 0)),
            pl.BlockSpec((None, n_k, MLA_HEADS * HEAD_PAD), lambda bi, i: (bi, k_row0 // n_k, 0),
                         pipeline_mode=pl.Buffered(1)),
            pl.BlockSpec((None, n_k, MLA_W), lambda bi, i: (bi, k_row0 // n_k, 0),
                         pipeline_mode=pl.Buffered(1)),
        ],
        out_specs=pl.BlockSpec((None, tq, MLA_W), lambda bi, i: (bi, i, 0)),
        compiler_params=_cparams(("parallel", "parallel")),
        name="mla_attention",
    )(qc, kc, v)


def _fourier_kernel(cm_ref, sm_ref, pq_ref, o_ref):
    y = _dot(cm_ref[...], pq_ref[:, 0:FNET_W]) + _dot(sm_ref[...], pq_ref[:, FNET_W:])
    o_ref[...] = y.astype(BF16)


def _fourier(pq, cm, sm, tm, row0):
    b = pq.shape[0]
    t = cm.shape[0]
    return pl.pallas_call(
        _fourier_kernel,
        out_shape=jax.ShapeDtypeStruct((b, t, FNET_W), BF16),
        grid=(t // tm, b),
        in_specs=[
            pl.BlockSpec((tm, t), lambda i, bi: (i, 0)),
            pl.BlockSpec((tm, t), lambda i, bi: (i, 0)),
            pl.BlockSpec((None, t, 2 * FNET_W), lambda i, bi: (bi, row0 // t, 0)),
        ],
        out_specs=pl.BlockSpec((None, tm, FNET_W), lambda i, bi: (bi, i, 0)),
        compiler_params=_cparams(("parallel", "parallel")),
        name="fourier_mix",
    )(cm, sm, pq)


def _gla_chunk(q, k, v, f, tri, ind_ref, dmask_ref, st_ref, reverse):
    c = GLA_CHUNK
    f0 = f.astype(BF16)
    r1 = f - f0.astype(F32)
    f1 = r1.astype(BF16)
    f2 = (r1 - f1.astype(F32)).astype(BF16)
    g = _dot(tri, f0) + _dot(tri, f1) + _dot(tri, f2)
    e = g[0:1, :] if reverse else g[c - 1:c, :]
    qf = q.astype(F32)
    kf = k.astype(F32)
    qg = (qf * jnp.exp(g)).astype(BF16)
    kd = (kf * jnp.exp(e - g)).astype(BF16)
    st = st_ref[...]
    st_b = st.astype(BF16)

    sb = GLA_SUB
    ind = ind_ref[...]
    row_sb = lax.broadcasted_iota(I32, (sb, GLA_QK_W), 0)
    row_c = lax.broadcasted_iota(I32, (c, GLA_QK_W), 0)
    a_rows = []
    for i in range(c // sb):
        lo = i * sb
        gb, kb, qb = g[lo:lo + sb], kf[lo:lo + sb], qf[lo:lo + sb]
        ws = []
        for t in range(sb):
            w = jnp.exp(jnp.minimum(gb[t:t + 1] - gb, 0.0)) * kb * qb[t:t + 1]
            keep = (row_sb >= t) if reverse else (row_sb <= t)
            ws.append(jnp.where(keep, w, 0.0))
        w_all = jnp.concatenate(ws, axis=0).astype(BF16)
        r = _dot(w_all, ind) * dmask_ref[i]
        a_i = jnp.sum(r.reshape(sb, sb, GLA_QK_W), axis=1)
        ref_row = lo + sb if reverse else lo - 1
        if 0 <= ref_row < c:
            gr = g[ref_row:ref_row + 1]
            qt = (qb * jnp.exp(gb - gr)).astype(BF16)
            valid = (row_c >= lo + sb) if reverse else (row_c < lo)
            kt = jnp.where(valid, kf * jnp.exp(jnp.minimum(gr - g, 0.0)), 0.0).astype(BF16)
            kbd = jnp.tile(kt, (GLA_HEADS, 1)) * ind
            a_i = a_i + _dot_nt(qt, kbd)
        a_rows.append(a_i)
    a = jnp.concatenate(a_rows, axis=0).astype(BF16)

    outs, upd = [], []
    for hd in range(GLA_HEADS):
        ks_ = slice(hd * GLA_DK, (hd + 1) * GLA_DK)
        vh = v[:, hd * GLA_DV:(hd + 1) * GLA_DV]
        o_inter = _dot_nt(qg[:, ks_], st_b[:, ks_])
        o_intra = _dot(a[:, ks_], vh)
        outs.append(o_inter + o_intra)
        upd.append(_dot_tn(vh, kd[:, ks_]))
    st_ref[...] = st * jnp.exp(e) + jnp.concatenate(upd, axis=1)
    return jnp.concatenate(outs, axis=1)


def _gla_kernel(qf_ref, kf_ref, vf_ref, ff_ref, qb_ref, kb_ref, vb_ref, fb_ref,
                tril_ref, triu_ref, ind_ref, dmask_ref, of_ref, ob_ref,
                stf, stb, *, n_chunks):
    @pl.when(pl.program_id(1) == 0)
    def _():
        stf[...] = jnp.zeros_like(stf)
        stb[...] = jnp.zeros_like(stb)

    c = GLA_CHUNK
    for j in range(n_chunks):
        rows = slice(j * c, (j + 1) * c)
        of_ref[rows, :] = _gla_chunk(qf_ref[rows, :], kf_ref[rows, :], vf_ref[rows, :], ff_ref[rows, :],
                                     tril_ref[...], ind_ref, dmask_ref, stf, False)
    for j in reversed(range(n_chunks)):
        rows = slice(j * c, (j + 1) * c)
        ob_ref[rows, :] = _gla_chunk(qb_ref[rows, :], kb_ref[rows, :], vb_ref[rows, :], fb_ref[rows, :],
                                     triu_ref[...], ind_ref, dmask_ref, stb, True)


def _gla(gq, gk, gv, lfb, tb, seq, ctx):
    b, t_all, _ = gq.shape
    nlb, ncb = seq // tb, ctx // tb
    c = GLA_CHUNK

    def fwd(i):
        return jnp.where(i < ncb, nlb + i, i - ncb)

    def bwd(i):
        return jnp.where(i < ncb, nlb + ncb - 1 - i, nlb - 1 - (i - ncb))

    def spec(w, order, col=0):
        return pl.BlockSpec((None, tb, w), lambda bi, i: (bi, order(i), col))

    r = np.arange(c)
    tril = jnp.asarray(r[:, None] >= r[None, :], BF16)
    triu = jnp.asarray(r[:, None] <= r[None, :], BF16)
    hs = np.arange(GLA_QK_W)
    ind = jnp.asarray(hs[:, None] // GLA_DK == hs[None, :] // GLA_DK, BF16)
    sb = GLA_SUB
    s_loc = np.tile(np.arange(sb), sb)
    dmask = jnp.asarray(np.stack([(i * sb + s_loc)[:, None] == (hs % GLA_DK)[None, :] for i in range(c // sb)]),
                        F32)

    def cspec(a):
        return pl.BlockSpec(a.shape, lambda bi, i: (0,) * a.ndim)

    return pl.pallas_call(
        functools.partial(_gla_kernel, n_chunks=tb // c),
        out_shape=[jax.ShapeDtypeStruct((b, t_all, GLA_W), F32)] * 2,
        grid=(b, nlb + ncb),
        in_specs=[spec(GLA_QK_W, fwd), spec(GLA_QK_W, fwd), spec(GLA_W, fwd), spec(GLA_QK_W, fwd, 0),
                  spec(GLA_QK_W, bwd), spec(GLA_QK_W, bwd), spec(GLA_W, bwd), spec(GLA_QK_W, bwd, 1),
                  cspec(tril), cspec(triu), cspec(ind), cspec(dmask)],
        out_specs=[spec(GLA_W, fwd), spec(GLA_W, bwd)],
        scratch_shapes=[pltpu.VMEM((GLA_DV, GLA_QK_W), F32), pltpu.VMEM((GLA_DV, GLA_QK_W), F32)],
        compiler_params=_cparams(("parallel", "arbitrary")),
        name="gla_scan",
    )(gq, gk, gv, lfb, gq, gk, gv, lfb, tril, triu, ind, dmask)


def _merge_kernel(x_ref, mod_ref, oml_ref, omc_ref, yfl_ref, yfc_ref, of_ref, ob_ref, og_ref, gate_ref,
                  gng_ref, wbm_ref, wbf_ref, wbg_ref, wo_ref, g2_ref, rw_ref, rb_ref,
                  x_out, h_out, idx_out, gw_out, *, n_lat_tiles):
    d = D_MODEL
    m = mod_ref[...]
    is_lat = pl.program_id(1) < n_lat_tiles
    y_fn = jnp.where(is_lat, yfl_ref[...], yfc_ref[...])
    o_mla = jnp.where(is_lat, oml_ref[...], omc_ref[...])
    o = of_ref[...] + ob_ref[...]
    gng = gng_ref[...]
    og = og_ref[...].astype(F32)
    parts = []
    for hd in range(GLA_HEADS):
        sl = slice(hd * GLA_DV, (hd + 1) * GLA_DV)
        parts.append(_rms(o[:, sl], gng[:, sl]))
    o_gla = (jnp.concatenate(parts, axis=1) * (og * jax.nn.sigmoid(og))).astype(BF16)

    gate = jax.nn.sigmoid(gate_ref[...].astype(F32))
    y = (gate[:, 0:d] * _dot(o_mla, wbm_ref[...])
         + gate[:, d:2 * d] * _dot(y_fn, wbf_ref[...])
         + gate[:, 2 * d:3 * d] * _dot(o_gla, wbg_ref[...]))
    x1 = x_ref[...] + m[:, 2 * d:3 * d] * _dot(y.astype(BF16), wo_ref[...])
    x_out[...] = x1
    h2 = _rms(x1, g2_ref[...]) * (1.0 + m[:, 4 * d:5 * d]) + m[:, 3 * d:4 * d]
    _store_row_tiles(h_out, _pack_bf16_pairs(h2))

    logits = _dot(h2.astype(BF16), rw_ref[...]) + rb_ref[...]
    lane = lax.broadcasted_iota(I32, logits.shape, 1)
    lane_f = lane.astype(F32)
    neg = jnp.float32(-jnp.inf)
    lg = jnp.where(lane < N_EXPERTS, logits, neg)
    idx_acc = jnp.zeros(logits.shape, I32)
    val_acc = jnp.zeros(logits.shape, F32)
    v0 = None
    for kk in range(TOP_K):
        mx = jnp.max(lg, axis=-1, keepdims=True)
        ix = jnp.min(jnp.where(lg == mx, lane_f, float(ROUTER_PAD)), axis=-1, keepdims=True).astype(I32)
        if kk == 0:
            v0 = mx
        idx_acc = jnp.where(lane == kk, ix, idx_acc)
        val_acc = jnp.where(lane == kk, jnp.exp(mx - v0), val_acc)
        lg = jnp.where(lane == ix, neg, lg)
    idx_out[...] = idx_acc
    gw_out[...] = val_acc / jnp.sum(val_acc, axis=-1, keepdims=True)


def _merge(xa, mod, o_mla_lat, o_mla_ctx, y_fn_lat, y_fn_ctx, o_f, o_b, og, gate, lw, tm, n_lat_tiles):
    b, t_all, d = xa.shape

    def tspec(w):
        return pl.BlockSpec((None, tm, w), lambda bi, i: (bi, i, 0))

    def lat_spec(w):
        return pl.BlockSpec((None, tm, w), lambda bi, i: (bi, jnp.minimum(i, n_lat_tiles - 1), 0))

    def ctx_spec(w):
        return pl.BlockSpec((None, tm, w), lambda bi, i: (bi, jnp.maximum(i - n_lat_tiles, 0), 0))

    def cspec(a):
        return pl.BlockSpec(a.shape, lambda bi, i: (0,) * a.ndim)

    consts = [lw["gla_norm_g"], lw["w_br_mla"], lw["w_br_fnet"], lw["w_br_gla"], lw["w_o"],
              lw["norm2_g"], lw["router_w"], lw["router_b"]]
    return pl.pallas_call(
        functools.partial(_merge_kernel, n_lat_tiles=n_lat_tiles),
        out_shape=[jax.ShapeDtypeStruct((b, t_all, d), F32), jax.ShapeDtypeStruct((b, t_all, ROW_TILES, 128), U32),
                   jax.ShapeDtypeStruct((b, t_all, ROUTER_PAD), I32),
                   jax.ShapeDtypeStruct((b, t_all, ROUTER_PAD), F32)],
        grid=(b, t_all // tm),
        in_specs=[
            tspec(d),
            pl.BlockSpec((None, None, 1, 6 * d), lambda bi, i: (bi, jnp.where(i < n_lat_tiles, 0, 1), 0, 0)),
            lat_spec(MLA_W), ctx_spec(MLA_W), lat_spec(FNET_W), ctx_spec(FNET_W),
            tspec(GLA_W), tspec(GLA_W), tspec(GLA_W), tspec(3 * d),
        ] + [cspec(a) for a in consts],
        out_specs=[tspec(d), pl.BlockSpec((None, tm, ROW_TILES, 128), lambda bi, i: (bi, i, 0, 0)),
                   tspec(ROUTER_PAD), tspec(ROUTER_PAD)],
        input_output_aliases={0: 0},
        compiler_params=_cparams(("parallel", "parallel")),
        name="merge_router",
    )(xa, mod, o_mla_lat, o_mla_ctx, y_fn_lat, y_fn_ctx, o_f, o_b, og, gate, *consts)


def _row_gather(idx_smem, slot, src_hbm, dst, sem, lo, hi):
    for r in range(lo, hi):
        pltpu.make_async_copy(src_hbm.at[idx_smem[slot, r]], dst.at[slot, pl.ds(r * ROW_TILES, ROW_TILES)],
                              sem.at[slot]).start()


def _gather_step(i, n, idx_hbm, src_hbm, idx_smem, buf, isem, gsem, n_rows):
    slot = lax.rem(i, 2)
    nxt = 1 - slot
    last = n - 1

    def idx_copy(blk, s):
        return pltpu.make_async_copy(idx_hbm.at[blk], idx_smem.at[s], isem.at[s])

    def rows_wait(s):
        pltpu.make_async_copy(buf.at[s], buf.at[s], gsem.at[s]).wait()

    @pl.when(i == 0)
    def _():
        idx_copy(0, 0).start()
        idx_copy(0, 0).wait()
        _row_gather(idx_smem, 0, src_hbm, buf, gsem, 0, n_rows)
        idx_copy(jnp.minimum(1, last), 1).start()

    idx_copy(jnp.minimum(i + 1, last), nxt).wait()
    rows_wait(slot)

    def issue(lo, hi):
        _row_gather(idx_smem, nxt, src_hbm, buf, gsem, lo, hi)

    def finish():
        idx_copy(jnp.minimum(i + 2, last), slot).start()

        @pl.when(i == last)
        def _():
            rows_wait(nxt)
            idx_copy(last, slot).wait()

    return slot, issue, finish


MOE_ISSUE_CHUNKS = 4


def _moe_kernel(blk_e_ref, tok_hbm, h_hbm, wu_ref, bu_ref, wd_ref, bd_ref, y_out,
                idx_smem, xbuf, isem, gsem, wu_b, wd_b):
    i = pl.program_id(0)

    @pl.when((i == 0) | (blk_e_ref[i] != blk_e_ref[jnp.maximum(i - 1, 0)]))
    def _():
        wu_b[...] = wu_ref[...].astype(BF16)
        wd_b[...] = wd_ref[...].astype(BF16)

    slot, issue, finish = _gather_step(i, pl.num_programs(0), tok_hbm, h_hbm, idx_smem, xbuf, isem, gsem,
                                       MOE_BLOCK)
    xb = jnp.concatenate(_unpack_bf16_pairs(_load_row_tiles(xbuf, slot, 0, MOE_BLOCK)), axis=1).astype(BF16)
    nc = MOE_ISSUE_CHUNKS
    wc = D_EXPERT // nc
    rows_per = MOE_BLOCK // (2 * nc)
    acts = []
    for j in range(nc):
        glu = _dot(xb, wu_b[:, j * wc:(j + 1) * wc]) + bu_ref[:, j * wc:(j + 1) * wc]
        lin = (_dot(xb, wu_b[:, D_EXPERT + j * wc:D_EXPERT + (j + 1) * wc])
               + bu_ref[:, D_EXPERT + j * wc:D_EXPERT + (j + 1) * wc])
        glu = jnp.minimum(glu, SWIGLU_LIMIT)
        lin = jnp.clip(lin, -SWIGLU_LIMIT, SWIGLU_LIMIT)
        acts.append((glu * jax.nn.sigmoid(SWIGLU_ALPHA * glu) * (lin + 1.0)).astype(BF16))
        issue(j * rows_per, (j + 1) * rows_per)
    act = jnp.concatenate(acts, axis=1)
    ys = []
    for j in range(nc):
        ys.append(_dot(act, wd_b[:, j * wc:(j + 1) * wc]) + bd_ref[:, j * wc:(j + 1) * wc])
        issue((nc + j) * rows_per, (nc + j + 1) * rows_per)
    _store_row_tiles(y_out, _pack_bf16_pairs(jnp.concatenate(ys, axis=1)))
    finish()


def _moe_experts(blk_e, row_tok, h_flat, w_up, b_up, w_down, b_down, layer):
    n_blocks = blk_e.shape[0]
    d = D_MODEL
    grid_spec = pltpu.PrefetchScalarGridSpec(
        num_scalar_prefetch=1,
        grid=(n_blocks,),
        in_specs=[
            pl.BlockSpec(memory_space=pl.ANY),
            pl.BlockSpec(memory_space=pl.ANY),
            pl.BlockSpec((None, None, d, 2 * D_EXPERT), lambda i, be: (layer, be[i], 0, 0)),
            pl.BlockSpec((None, None, 1, 2 * D_EXPERT), lambda i, be: (layer, be[i], 0, 0)),
            pl.BlockSpec((None, None, D_EXPERT, d), lambda i, be: (layer, be[i], 0, 0)),
            pl.BlockSpec((None, None, 1, d), lambda i, be: (layer, be[i], 0, 0)),
        ],
        out_specs=pl.BlockSpec((MOE_BLOCK, ROW_TILES, 128), lambda i, be: (i, 0, 0)),
        scratch_shapes=[pltpu.SMEM((2, MOE_BLOCK), I32), pltpu.VMEM((2, MOE_BLOCK * ROW_TILES, 128), U32),
                        pltpu.SemaphoreType.DMA((2,)), pltpu.SemaphoreType.DMA((2,)),
                        pltpu.VMEM((d, 2 * D_EXPERT), BF16), pltpu.VMEM((D_EXPERT, d), BF16)],
    )
    return pl.pallas_call(
        _moe_kernel,
        out_shape=jax.ShapeDtypeStruct((n_blocks * MOE_BLOCK, ROW_TILES, 128), U32),
        grid_spec=grid_spec,
        compiler_params=_cparams(("arbitrary",)),
        name="moe_experts",
    )(blk_e, row_tok, h_flat, w_up, b_up, w_down, b_down)


def _combine_kernel(pos_hbm, y_hbm, x_ref, mod_ref, gw_ref, x_out, idx_smem, ybuf, isem, gsem, *, tm, n_tiles):
    bi = pl.program_id(0)
    ti = pl.program_id(1)
    i = bi * n_tiles + ti
    n = pl.num_programs(0) * n_tiles
    slot, issue, finish = _gather_step(i, n, pos_hbm, y_hbm, idx_smem, ybuf, isem, gsem, tm * TOP_K)
    gw = gw_ref[...]
    acc_lo = acc_hi = None
    for kk in range(TOP_K):
        lo, hi = _unpack_bf16_pairs(_load_row_tiles(ybuf, slot, kk * tm, tm))
        gk = gw[:, kk:kk + 1]
        acc_lo = lo * gk if kk == 0 else acc_lo + lo * gk
        acc_hi = hi * gk if kk == 0 else acc_hi + hi * gk
        issue(kk * tm, (kk + 1) * tm)
    d = D_MODEL
    x_out[...] = x_ref[...] + mod_ref[:, 5 * d:6 * d] * jnp.concatenate([acc_lo, acc_hi], axis=1)
    finish()


def _combine(xa, mod, gates, pos, y_sorted, tm, n_lat_tiles):
    b, t_all, d = xa.shape
    n_tiles = t_all // tm
    return pl.pallas_call(
        functools.partial(_combine_kernel, tm=tm, n_tiles=n_tiles),
        out_shape=jax.ShapeDtypeStruct((b, t_all, d), F32),
        grid=(b, n_tiles),
        in_specs=[
            pl.BlockSpec(memory_space=pl.ANY),
            pl.BlockSpec(memory_space=pl.ANY),
            pl.BlockSpec((None, tm, d), lambda bi, i: (bi, i, 0)),
            pl.BlockSpec((None, None, 1, 6 * d), lambda bi, i: (bi, jnp.where(i < n_lat_tiles, 0, 1), 0, 0)),
            pl.BlockSpec((None, tm, ROUTER_PAD), lambda bi, i: (bi, i, 0)),
        ],
        out_specs=pl.BlockSpec((None, tm, d), lambda bi, i: (bi, i, 0)),
        scratch_shapes=[pltpu.SMEM((2, tm * TOP_K), I32), pltpu.VMEM((2, tm * TOP_K * ROW_TILES, 128), U32),
                        pltpu.SemaphoreType.DMA((2,)), pltpu.SemaphoreType.DMA((2,))],
        input_output_aliases={2: 0},
        compiler_params=_cparams(("arbitrary", "arbitrary")),
        name="moe_combine",
    )(pos, y_sorted, xa, mod, gates)


def _final_kernel(x_ref, g_ref, o_ref):
    o_ref[...] = _rms(x_ref[...], g_ref[...])


def _final_norm(xa, g, seq, tm):
    b, _, d = xa.shape
    return pl.pallas_call(
        _final_kernel,
        out_shape=jax.ShapeDtypeStruct((b, seq, d), F32),
        grid=(b, seq // tm),
        in_specs=[pl.BlockSpec((None, tm, d), lambda bi, i: (bi, i, 0)),
                  pl.BlockSpec((1, d), lambda bi, i: (0, 0))],
        out_specs=pl.BlockSpec((None, tm, d), lambda bi, i: (bi, i, 0)),
        compiler_params=_cparams(("parallel", "parallel")),
        name="final_norm",
    )(xa, g.reshape(1, d))


def _pad_cols(w, width):
    return jnp.pad(w, ((0, 0), (0, width - w.shape[1])))


def _rope_partner(w):
    a = ROPE_AXIS // 2
    return jnp.concatenate([-w[:, a:2 * a], w[:, 0:a], -w[:, 3 * a:4 * a], w[:, 2 * a:3 * a]], axis=1)


def _layer_weights(l, w_in, mla_q_norm_g, mla_w_uq, mla_kv_norm_g, mla_w_ukv, gla_w_gate_f, gla_b_gate_f,
                   gla_w_gate_b, gla_b_gate_b, gla_norm_g, w_br_mla, w_br_fnet, w_br_gla, w_o, norm1_g,
                   norm2_g, router_w, router_b):
    d = D_MODEL
    splits = np.cumsum([MLA_Q_RANK, MLA_KV_RANK + MLA_ROPE, FNET_W, GLA_QK_W, GLA_QK_W, GLA_W, GLA_W,
                        GLA_GATE_RANK, GLA_GATE_RANK])
    wq, wkv, wfn, wgq, wgk, wgv, wog, wgf, wgb, wgate = jnp.split(w_in[l], [int(s) for s in splits], axis=1)
    w_in_p = jnp.concatenate([wq, _pad_cols(wkv, 256), wfn, wgq, wgk, wgv, wog,
                              _pad_cols(jnp.concatenate([wgf, wgb], axis=1), 128), wgate], axis=1).astype(BF16)

    qk = MLA_NOPE + MLA_ROPE
    zq = jnp.zeros((MLA_Q_RANK, HEAD_PAD - qk), F32)
    zn = jnp.zeros((MLA_Q_RANK, MLA_NOPE), F32)
    wqa, wqb = [], []
    for hd in range(MLA_HEADS):
        wh = mla_w_uq[l][:, hd * qk:(hd + 1) * qk]
        wqa.append(jnp.concatenate([wh, zq], axis=1))
        wqb.append(jnp.concatenate([zn, _rope_partner(wh[:, MLA_NOPE:]), zq], axis=1))
    eye = jnp.eye(MLA_ROPE, dtype=F32)
    wka, wkb, wv = [], [], []
    for hd in range(MLA_HEADS):
        wh = mla_w_ukv[l][:, hd * (MLA_NOPE + MLA_V):(hd + 1) * (MLA_NOPE + MLA_V)]
        top = jnp.concatenate([wh[:, 0:MLA_NOPE], jnp.zeros((MLA_KV_RANK, HEAD_PAD - MLA_NOPE), F32)], axis=1)
        mid_a = jnp.concatenate([jnp.zeros((MLA_ROPE, MLA_NOPE), F32), eye,
                                 jnp.zeros((MLA_ROPE, HEAD_PAD - qk), F32)], axis=1)
        mid_b = jnp.concatenate([jnp.zeros((MLA_ROPE, MLA_NOPE), F32), _rope_partner(eye),
                                 jnp.zeros((MLA_ROPE, HEAD_PAD - qk), F32)], axis=1)
        bot = jnp.zeros((256 - MLA_KV_RANK - MLA_ROPE, HEAD_PAD), F32)
        wka.append(jnp.concatenate([top, mid_a, bot], axis=0))
        wkb.append(jnp.concatenate([jnp.zeros_like(top), mid_b, bot], axis=0))
        wv.append(jnp.concatenate([wh[:, MLA_NOPE:], jnp.zeros((256 - MLA_KV_RANK, MLA_V), F32)], axis=0))

    ch = np.arange(FNET_GROUP_W)
    ang = 2.0 * np.pi * ((ch[:, None] * ch[None, :]) % FNET_GROUP_W) / FNET_GROUP_W
    wc = np.concatenate([np.cos(ang), np.sin(ang)], axis=1) / math.sqrt(FNET_GROUP_W)

    wg = jnp.zeros((128, 2 * GLA_QK_W), F32)
    wg = wg.at[0:GLA_GATE_RANK, 0:GLA_QK_W].set(gla_w_gate_f[l])
    wg = wg.at[GLA_GATE_RANK:2 * GLA_GATE_RANK, GLA_QK_W:].set(gla_w_gate_b[l])

    return {
        "norm1_g": norm1_g[l].reshape(1, d),
        "w_in": w_in_p,
        "q_norm_g": mla_q_norm_g[l].reshape(1, -1),
        "wqa": jnp.concatenate(wqa, axis=1).astype(BF16),
        "wqb": jnp.concatenate(wqb, axis=1).astype(BF16),
        "kv_norm_g": mla_kv_norm_g[l].reshape(1, -1),
        "wka": jnp.concatenate(wka, axis=1).astype(BF16),
        "wkb": jnp.concatenate(wkb, axis=1).astype(BF16),
        "wv": jnp.concatenate(wv, axis=1).astype(BF16),
        "wc": jnp.asarray(wc, BF16),
        "wg": wg.astype(BF16),
        "bg": jnp.concatenate([gla_b_gate_f[l], gla_b_gate_b[l]]).reshape(1, -1),
        "gla_norm_g": gla_norm_g[l].reshape(1, -1),
        "w_br_mla": w_br_mla[l].astype(BF16),
        "w_br_fnet": w_br_fnet[l].astype(BF16),
        "w_br_gla": w_br_gla[l].astype(BF16),
        "w_o": w_o[l].astype(BF16),
        "norm2_g": norm2_g[l].reshape(1, d),
        "router_w": _pad_cols(router_w[l], ROUTER_PAD).astype(BF16),
        "router_b": _pad_cols(router_b[l].reshape(1, -1), ROUTER_PAD),
    }


def _rope_tables(seq, ctx):
    rows = seq // GRID_W
    row = jnp.repeat(jnp.arange(rows, dtype=F32), GRID_W)
    col = jnp.tile(jnp.arange(GRID_W, dtype=F32), rows)
    inv_freq = ROPE_BASE ** (-jnp.arange(0, ROPE_AXIS, 2, dtype=F32) / ROPE_AXIS)
    ang_r = row[:, None] * inv_freq
    ang_c = col[:, None] * inv_freq
    ang = jnp.concatenate([ang_r, ang_r, ang_c, ang_c], axis=1)
    ang = jnp.concatenate([ang, jnp.zeros((ctx, MLA_ROPE), F32)], axis=0)
    t_all = seq + ctx
    pad = jnp.zeros((t_all, HEAD_PAD - MLA_NOPE - MLA_ROPE), F32)
    cos_t = jnp.concatenate([jnp.ones((t_all, MLA_NOPE), F32), jnp.cos(ang), pad], axis=1)
    sin_t = jnp.concatenate([jnp.zeros((t_all, MLA_NOPE), F32), jnp.sin(ang), pad], axis=1)
    return cos_t, sin_t


def _dft_mats(t):
    r = np.arange(t, dtype=np.int64)
    ang = 2.0 * np.pi * ((r[:, None] * r[None, :]) % t) / t
    s = 1.0 / math.sqrt(t)
    return jnp.asarray(np.cos(ang) * s, BF16), jnp.asarray(-np.sin(ang) * s, BF16)


def _routing(top_idx, n_tok):
    n_assign = n_tok * TOP_K
    flat_e = top_idx.reshape(n_assign)
    rt = 1024
    assert n_assign % rt == 0
    onehot = (flat_e[:, None] == jnp.arange(N_EXPERTS, dtype=I32)[None, :])
    oh3 = onehot.reshape(n_assign // rt, rt, N_EXPERTS).astype(BF16)
    tri = jnp.asarray(np.tril(np.ones((rt, rt), np.float32)), BF16)
    within = jnp.einsum("ts,nse->nte", tri, oh3, preferred_element_type=F32)
    tile_cnt = within[:, -1, :]
    tile_off = jnp.cumsum(tile_cnt, axis=0) - tile_cnt
    csum = (within + tile_off[:, None, :]).reshape(n_assign, N_EXPERTS)
    counts = (tile_off[-1] + tile_cnt[-1]).astype(I32)
    rank = jnp.sum(jnp.where(onehot, csum, 0.0), axis=1).astype(I32) - 1
    padded = ((counts + MOE_BLOCK - 1) // MOE_BLOCK) * MOE_BLOCK
    pad_end = jnp.cumsum(padded)
    pad_start = pad_end - padded
    dest = (pad_start[flat_e] + rank).astype(I32)
    n_blocks = -(-n_assign // MOE_BLOCK) + N_EXPERTS
    n_rows = n_blocks * MOE_BLOCK
    row_tok = jnp.zeros((n_rows,), I32).at[dest].set(jnp.arange(n_assign, dtype=I32) // TOP_K,
                                                     unique_indices=True)
    blk_start = jnp.arange(n_blocks, dtype=I32) * MOE_BLOCK
    blk_e = jnp.minimum(jnp.sum((pad_end[None, :] <= blk_start[:, None]).astype(I32), axis=1), N_EXPERTS - 1)
    return blk_e, row_tok.reshape(n_blocks, MOE_BLOCK), dest


def kernel(x, c, ctx, c_ctx, w_mod, b_mod, norm1_g, w_in, mla_q_norm_g, mla_w_uq, mla_kv_norm_g, mla_w_ukv,
           gla_w_gate_f, gla_b_gate_f, gla_w_gate_b, gla_b_gate_b, gla_norm_g, w_br_mla, w_br_fnet, w_br_gla,
           w_o, norm2_g, router_w, router_b, exp_w_up, exp_b_up, exp_w_down, exp_b_down, final_norm_g):
    b, seq, d = x.shape
    n_ctx = ctx.shape[1]
    n_layers = w_mod.shape[0]
    t_all = seq + n_ctx
    assert d == D_MODEL and seq % n_ctx == 0 and seq % GRID_W == 0 and n_ctx % GLA_CHUNK == 0
    tm = min(256, n_ctx)
    assert n_ctx % tm == 0 and seq % tm == 0 and (tm * TOP_K) % MOE_BLOCK == 0
    n_lat_tiles = seq // tm
    t_fourier = min(512, seq)
    t_attn = min(512, seq)

    xa = jnp.concatenate([x, ctx], axis=1)

    rows = -(-(b + 1) // 8) * 8
    cv = jnp.zeros((rows, d), F32).at[0:b].set(c).at[b].set(c_ctx)
    mod_all = _mod_all(cv, w_mod, b_mod)
    cos_t, sin_t = _rope_tables(seq, n_ctx)
    cm_l, sm_l = _dft_mats(seq)
    cm_c, sm_c = _dft_mats(n_ctx)

    n_tok = b * t_all
    b_up4 = exp_b_up.reshape(n_layers, N_EXPERTS, 1, 2 * D_EXPERT)
    b_down4 = exp_b_down.reshape(n_layers, N_EXPERTS, 1, d)
    for l in range(n_layers):
        lw = _layer_weights(l, w_in, mla_q_norm_g, mla_w_uq, mla_kv_norm_g, mla_w_ukv, gla_w_gate_f,
                            gla_b_gate_f, gla_w_gate_b, gla_b_gate_b, gla_norm_g, w_br_mla, w_br_fnet,
                            w_br_gla, w_o, norm1_g, norm2_g, router_w, router_b)
        mod_lat = mod_all[l, 0:b]
        mod_ctx = jnp.broadcast_to(mod_all[l, b][None, :], (b, 6 * d))
        mod = jnp.stack([mod_lat, mod_ctx], axis=1).reshape(b, 2, 1, 6 * d)

        qc, kc, v, pq, gq, gk, gv, lfb, og, gate = _inproj(xa, mod, cos_t, sin_t, lw, tm, n_lat_tiles)
        o_mla_lat = _attention(qc, kc, v, t_attn, 0, seq, 0, t_all)
        o_mla_ctx = _attention(qc, kc, v, tm, seq, n_ctx, seq, n_ctx)
        y_fn_lat = _fourier(pq, cm_l, sm_l, t_fourier, 0)
        y_fn_ctx = _fourier(pq, cm_c, sm_c, n_ctx, seq)
        o_f, o_b = _gla(gq, gk, gv, lfb, tm, seq, n_ctx)
        xa, h2, top_idx, gates = _merge(xa, mod, o_mla_lat, o_mla_ctx, y_fn_lat, y_fn_ctx, o_f, o_b, og, gate,
                                        lw, tm, n_lat_tiles)

        blk_e, row_tok, dest = _routing(top_idx[..., 0:TOP_K], n_tok)
        y_sorted = _moe_experts(blk_e, row_tok, h2.reshape(n_tok, ROW_TILES, 128), exp_w_up, b_up4, exp_w_down,
                                b_down4, l)
        pos = dest.reshape(n_tok // tm, tm, TOP_K).transpose(0, 2, 1).reshape(n_tok // tm, tm * TOP_K)
        xa = _combine(xa, mod, gates, pos, y_sorted, tm, n_lat_tiles)

    return _final_norm(xa, final_norm_g, seq, tm)
```

```python
import functools
import math

import jax
import jax.numpy as jnp
import numpy as np
from jax import lax
from jax.experimental import pallas as pl
from jax.experimental.pallas import tpu as pltpu

F32 = jnp.float32
BF16 = jnp.bfloat16
I32 = jnp.int32
U32 = jnp.uint32

D_MODEL = 1024
GRID_W = 64
EPS = 1e-6

MLA_HEADS = 8
MLA_Q_RANK = 256
MLA_KV_RANK = 128
MLA_NOPE = 64
MLA_ROPE = 32
MLA_V = 64
MLA_SCALE = (MLA_NOPE + MLA_ROPE) ** -0.5
ROPE_AXIS = MLA_ROPE // 2
ROPE_BASE = 10000.0
HEAD_PAD = 128

FNET_GROUPS = 4
FNET_GROUP_W = 128
FNET_W = FNET_GROUPS * FNET_GROUP_W

GLA_HEADS = 4
GLA_DK = 64
GLA_DV = 128
GLA_GATE_RANK = 16
GLA_TAU = 16.0
GLA_CHUNK = 64
GLA_SUB = 16
GLA_SCALE = GLA_DK ** -0.5
GLA_QK_W = GLA_HEADS * GLA_DK
GLA_W = GLA_HEADS * GLA_DV

N_EXPERTS = 32
TOP_K = 4
D_EXPERT = D_MODEL
SWIGLU_ALPHA = 1.702
SWIGLU_LIMIT = 7.0
MOE_BLOCK = 256
ROUTER_PAD = 128

MLA_W = MLA_HEADS * MLA_V

VMEM_LIMIT_V7X = 56 * 1024 * 1024

C_Q = 0
C_KV = C_Q + MLA_Q_RANK
C_FN = C_KV + 256
C_GQ = C_FN + FNET_W
C_GK = C_GQ + GLA_QK_W
C_GV = C_GK + GLA_QK_W
C_OG = C_GV + GLA_W
C_GG = C_OG + GLA_W
C_GATE = C_GG + 128
IN_WIDTH_PAD = C_GATE + 3 * D_MODEL


def _cparams(sem, vmem=VMEM_LIMIT_V7X):
    return pltpu.CompilerParams(dimension_semantics=sem, vmem_limit_bytes=vmem)


def _rms(x, g):
    return x * lax.rsqrt(jnp.mean(x * x, axis=-1, keepdims=True) + EPS) * g


def _dot(a, b):
    return jnp.dot(a, b, preferred_element_type=F32)


def _dot_nt(a, b):
    return lax.dot_general(a, b, (((1,), (1,)), ((), ())), preferred_element_type=F32)


def _dot_tn(a, b):
    return lax.dot_general(a, b, (((0,), (0,)), ((), ())), preferred_element_type=F32)


def _pack_bf16_pairs(x):
    w = x.shape[1] // 2
    r = lax.bitcast_convert_type(x.astype(BF16).astype(F32), U32)
    return r[:, w:] | (r[:, :w] >> 16)


def _unpack_bf16_pairs(u):
    lo = lax.bitcast_convert_type(u << 16, F32)
    hi = lax.bitcast_convert_type(u & jnp.uint32(0xFFFF0000), F32)
    return lo, hi


ROW_TILES = D_MODEL // 2 // 128


def _store_row_tiles(ref, x):
    for j in range(ROW_TILES):
        ref[:, j, :] = x[:, j * 128:(j + 1) * 128]


def _load_row_tiles(ref):
    return jnp.concatenate([ref[:, j, :] for j in range(ROW_TILES)], axis=1)


def _mod_kernel(cv_ref, w_ref, b_ref, o_ref):
    cv = cv_ref[...]
    s = cv * jax.nn.sigmoid(cv)
    o_ref[...] = jnp.dot(s, w_ref[...], preferred_element_type=F32,
                         precision=lax.Precision.HIGHEST) + b_ref[...]


def _mod_all(cv, w_mod, b_mod):
    n_layers, d, w6 = w_mod.shape
    rows = cv.shape[0]
    tn = 1536
    return pl.pallas_call(
        _mod_kernel,
        out_shape=jax.ShapeDtypeStruct((n_layers, rows, w6), F32),
        grid=(n_layers, w6 // tn),
        in_specs=[
            pl.BlockSpec((rows, d), lambda l, j: (0, 0)),
            pl.BlockSpec((None, d, tn), lambda l, j: (l, 0, j)),
            pl.BlockSpec((None, 1, tn), lambda l, j: (l, 0, j)),
        ],
        out_specs=pl.BlockSpec((None, rows, tn), lambda l, j: (l, 0, j)),
        compiler_params=_cparams(("parallel", "parallel")),
        name="adaln_mod",
    )(cv, w_mod, b_mod.reshape(n_layers, 1, w6))


def _inproj_kernel(x_ref, mod_ref, cos_ref, sin_ref, g1_ref, win_ref, qng_ref, wqa_ref, wqb_ref,
                   kvg_ref, wka_ref, wkb_ref, wv_ref, wc_ref, wg_ref, bg_ref,
                   q_out, k_out, v_out, pq_out, gq_out, gk_out, gv_out, lfb_out, og_out, gate_out):
    d = D_MODEL
    x = x_ref[...]
    m = mod_ref[...]
    h = (_rms(x, g1_ref[...]) * (1.0 + m[:, d:2 * d]) + m[:, 0:d]).astype(BF16)

    def proj(a, b):
        return _dot(h, win_ref[:, a:b])

    cos = jnp.tile(cos_ref[...], (1, MLA_HEADS))
    sin = jnp.tile(sin_ref[...], (1, MLA_HEADS))

    nq = _rms(proj(C_Q, C_KV), qng_ref[...]).astype(BF16)
    q = (_dot(nq, wqa_ref[...]) * cos + _dot(nq, wqb_ref[...]) * sin) * MLA_SCALE
    q_out[...] = q.astype(BF16)
    ukv = proj(C_KV, C_FN)
    ckn = _rms(ukv[:, 0:MLA_KV_RANK], kvg_ref[...])
    lhs = jnp.concatenate([ckn, ukv[:, MLA_KV_RANK:]], axis=1).astype(BF16)
    k_out[...] = (_dot(lhs, wka_ref[...]) * cos + _dot(lhs, wkb_ref[...]) * sin).astype(BF16)
    v_out[...] = _dot(lhs, wv_ref[...]).astype(BF16)

    ufn = proj(C_FN, C_GQ).astype(BF16)
    ps, qs = [], []
    for g in range(FNET_GROUPS):
        r = _dot(ufn[:, g * FNET_GROUP_W:(g + 1) * FNET_GROUP_W], wc_ref[...])
        ps.append(r[:, 0:FNET_GROUP_W])
        qs.append(r[:, FNET_GROUP_W:])
    pq_out[...] = jnp.concatenate(ps + qs, axis=1).astype(BF16)

    gq_out[...] = (proj(C_GQ, C_GK) * GLA_SCALE).astype(BF16)
    gk_out[...] = proj(C_GK, C_GV).astype(BF16)
    gv_out[...] = proj(C_GV, C_OG).astype(BF16)
    og_out[...] = proj(C_OG, C_GG).astype(BF16)
    z = _dot(proj(C_GG, C_GATE).astype(BF16), wg_ref[...]) + bg_ref[...]
    lfb_out[...] = (jnp.minimum(z, 0.0) - jnp.log1p(jnp.exp(-jnp.abs(z)))) * (1.0 / GLA_TAU)
    gate_out[...] = proj(C_GATE, IN_WIDTH_PAD).astype(BF16)


def _inproj(xa, mod, cos_t, sin_t, lw, tm, n_lat_tiles):
    b, t_all, d = xa.shape
    nt = t_all // tm

    def tok(w, dt):
        return jax.ShapeDtypeStruct((b, t_all, w), dt)

    def tspec(w):
        return pl.BlockSpec((None, tm, w), lambda bi, i: (bi, i, 0))

    def cspec(a):
        return pl.BlockSpec(a.shape, lambda bi, i: (0,) * a.ndim)

    consts = [lw["norm1_g"], lw["w_in"], lw["q_norm_g"], lw["wqa"], lw["wqb"], lw["kv_norm_g"],
              lw["wka"], lw["wkb"], lw["wv"], lw["wc"], lw["wg"], lw["bg"]]
    widths = [(1024, BF16), (1024, BF16), (MLA_W, BF16), (2 * FNET_W, BF16), (GLA_QK_W, BF16),
              (GLA_QK_W, BF16), (GLA_W, BF16), (2 * GLA_QK_W, F32), (GLA_W, BF16), (3 * D_MODEL, BF16)]
    return pl.pallas_call(
        _inproj_kernel,
        out_shape=[tok(w, dt) for w, dt in widths],
        grid=(b, nt),
        in_specs=[
            tspec(d),
            pl.BlockSpec((None, None, 1, 6 * d), lambda bi, i: (bi, jnp.where(i < n_lat_tiles, 0, 1), 0, 0)),
            pl.BlockSpec((tm, HEAD_PAD), lambda bi, i: (i, 0)),
            pl.BlockSpec((tm, HEAD_PAD), lambda bi, i: (i, 0)),
        ] + [cspec(a) for a in consts],
        out_specs=[tspec(w) for w, _ in widths],
        compiler_params=_cparams(("parallel", "parallel")),
        name="inproj",
    )(xa, mod, cos_t, sin_t, *consts)


def _attn_kernel(q_ref, k_ref, v_ref, o_ref):
    lane = lax.broadcasted_iota(I32, (q_ref.shape[0], 2 * MLA_V), 1)
    outs = []
    for hp in range(MLA_HEADS // 2):
        vp = v_ref[:, hp * 2 * MLA_V:(hp + 1) * 2 * MLA_V]
        rs = []
        for j in range(2):
            hd = 2 * hp + j
            q = q_ref[:, hd * HEAD_PAD:(hd + 1) * HEAD_PAD]
            k = k_ref[:, hd * HEAD_PAD:(hd + 1) * HEAD_PAD]
            s = _dot_nt(q, k)
            p = jnp.exp(s - jnp.max(s, axis=-1, keepdims=True))
            l = jnp.sum(p, axis=-1, keepdims=True)
            rs.append(_dot(p.astype(BF16), vp) / l)
        outs.append(jnp.where(lane < MLA_V, rs[0], rs[1]))
    o_ref[...] = jnp.concatenate(outs, axis=1).astype(BF16)


def _attention(qc, kc, v, tq, q_row0, n_q, k_row0, n_k):
    b = qc.shape[0]
    assert q_row0 % tq == 0 and n_q % tq == 0 and k_row0 % n_k == 0
    return pl.pallas_call(
        _attn_kernel,
        out_shape=jax.ShapeDtypeStruct((b, n_q, MLA_W), BF16),
        grid=(b, n_q // tq),
        in_specs=[
            pl.BlockSpec((None, tq, MLA_HEADS * HEAD_PAD), lambda bi, i: (bi, q_row0 // tq + i, 0)),
            pl.BlockSpec((None, n_k, MLA_HEADS * HEAD_PAD), lambda bi, i: (bi, k_row0 // n_k, 0),
                         pipeline_mode=pl.Buffered(1)),
            pl.BlockSpec((None, n_k, MLA_W), lambda bi, i: (bi, k_row0 // n_k, 0),
                         pipeline_mode=pl.Buffered(1)),
        ],
        out_specs=pl.BlockSpec((None, tq, MLA_W), lambda bi, i: (bi, i, 0)),
        compiler_params=_cparams(("parallel", "parallel")),
        name="mla_attention",
    )(qc, kc, v)


def _fourier_kernel(cm_ref, sm_ref, pq_ref, o_ref):
    y = _dot(cm_ref[...], pq_ref[:, 0:FNET_W]) + _dot(sm_ref[...], pq_ref[:, FNET_W:])
    o_ref[...] = y.astype(BF16)


def _fourier(pq, cm, sm, tm, row0):
    b = pq.shape[0]
    t = cm.shape[0]
    return pl.pallas_call(
        _fourier_kernel,
        out_shape=jax.ShapeDtypeStruct((b, t, FNET_W), BF16),
        grid=(t // tm, b),
        in_specs=[
            pl.BlockSpec((tm, t), lambda i, bi: (i, 0)),
            pl.BlockSpec((tm, t), lambda i, bi: (i, 0)),
            pl.BlockSpec((None, t, 2 * FNET_W), lambda i, bi: (bi, row0 // t, 0)),
        ],
        out_specs=pl.BlockSpec((None, tm, FNET_W), lambda i, bi: (bi, i, 0)),
        compiler_params=_cparams(("parallel", "parallel")),
        name="fourier_mix",
    )(cm, sm, pq)


def _gla_chunk(q, k, v, f, tri, ind_ref, dmask_ref, st_ref, reverse):
    c = GLA_CHUNK
    f0 = f.astype(BF16)
    r1 = f - f0.astype(F32)
    f1 = r1.astype(BF16)
    f2 = (r1 - f1.astype(F32)).astype(BF16)
    g = _dot(tri, f0) + _dot(tri, f1) + _dot(tri, f2)
    e = g[0:1, :] if reverse else g[c - 1:c, :]
    qf = q.astype(F32)
    kf = k.astype(F32)
    qg = (qf * jnp.exp(g)).astype(BF16)
    kd = (kf * jnp.exp(e - g)).astype(BF16)
    st = st_ref[...]
    st_b = st.astype(BF16)

    sb = GLA_SUB
    ind = ind_ref[...]
    row_sb = lax.broadcasted_iota(I32, (sb, GLA_QK_W), 0)
    row_c = lax.broadcasted_iota(I32, (c, GLA_QK_W), 0)
    a_rows = []
    for i in range(c // sb):
        lo = i * sb
        gb, kb, qb = g[lo:lo + sb], kf[lo:lo + sb], qf[lo:lo + sb]
        ws = []
        for t in range(sb):
            w = jnp.exp(jnp.minimum(gb[t:t + 1] - gb, 0.0)) * kb * qb[t:t + 1]
            keep = (row_sb >= t) if reverse else (row_sb <= t)
            ws.append(jnp.where(keep, w, 0.0))
        w_all = jnp.concatenate(ws, axis=0).astype(BF16)
        r = _dot(w_all, ind) * dmask_ref[i]
        a_i = jnp.sum(r.reshape(sb, sb, GLA_QK_W), axis=1)
        ref_row = lo + sb if reverse else lo - 1
        if 0 <= ref_row < c:
            gr = g[ref_row:ref_row + 1]
            qt = (qb * jnp.exp(gb - gr)).astype(BF16)
            valid = (row_c >= lo + sb) if reverse else (row_c < lo)
            kt = jnp.where(valid, kf * jnp.exp(jnp.minimum(gr - g, 0.0)), 0.0).astype(BF16)
            kbd = jnp.tile(kt, (GLA_HEADS, 1)) * ind
            a_i = a_i + _dot_nt(qt, kbd)
        a_rows.append(a_i)
    a = jnp.concatenate(a_rows, axis=0).astype(BF16)

    outs, upd = [], []
    for hd in range(GLA_HEADS):
        ks_ = slice(hd * GLA_DK, (hd + 1) * GLA_DK)
        vh = v[:, hd * GLA_DV:(hd + 1) * GLA_DV]
        o_inter = _dot_nt(qg[:, ks_], st_b[:, ks_])
        o_intra = _dot(a[:, ks_], vh)
        outs.append(o_inter + o_intra)
        upd.append(_dot_tn(vh, kd[:, ks_]))
    st_ref[...] = st * jnp.exp(e) + jnp.concatenate(upd, axis=1)
    return jnp.concatenate(outs, axis=1)


def _gla_kernel(qf_ref, kf_ref, vf_ref, ff_ref, qb_ref, kb_ref, vb_ref, fb_ref,
                tril_ref, triu_ref, ind_ref, dmask_ref, of_ref, ob_ref,
                stf, stb, *, n_chunks):
    @pl.when(pl.program_id(1) == 0)
    def _():
        stf[...] = jnp.zeros_like(stf)
        stb[...] = jnp.zeros_like(stb)

    c = GLA_CHUNK
    for j in range(n_chunks):
        rows = slice(j * c, (j + 1) * c)
        of_ref[rows, :] = _gla_chunk(qf_ref[rows, :], kf_ref[rows, :], vf_ref[rows, :], ff_ref[rows, :],
                                     tril_ref[...], ind_ref, dmask_ref, stf, False)
    for j in reversed(range(n_chunks)):
        rows = slice(j * c, (j + 1) * c)
        ob_ref[rows, :] = _gla_chunk(qb_ref[rows, :], kb_ref[rows, :], vb_ref[rows, :], fb_ref[rows, :],
                                     triu_ref[...], ind_ref, dmask_ref, stb, True)


def _gla(gq, gk, gv, lfb, tb, seq, ctx):
    b, t_all, _ = gq.shape
    nlb, ncb = seq // tb, ctx // tb
    c = GLA_CHUNK

    def fwd(i):
        return jnp.where(i < ncb, nlb + i, i - ncb)

    def bwd(i):
        return jnp.where(i < ncb, nlb + ncb - 1 - i, nlb - 1 - (i - ncb))

    def spec(w, order, col=0):
        return pl.BlockSpec((None, tb, w), lambda bi, i: (bi, order(i), col))

    r = np.arange(c)
    tril = jnp.asarray(r[:, None] >= r[None, :], BF16)
    triu = jnp.asarray(r[:, None] <= r[None, :], BF16)
    hs = np.arange(GLA_QK_W)
    ind = jnp.asarray(hs[:, None] // GLA_DK == hs[None, :] // GLA_DK, BF16)
    sb = GLA_SUB
    s_loc = np.tile(np.arange(sb), sb)
    dmask = jnp.asarray(np.stack([(i * sb + s_loc)[:, None] == (hs % GLA_DK)[None, :] for i in range(c // sb)]),
                        F32)

    def cspec(a):
        return pl.BlockSpec(a.shape, lambda bi, i: (0,) * a.ndim)

    return pl.pallas_call(
        functools.partial(_gla_kernel, n_chunks=tb // c),
        out_shape=[jax.ShapeDtypeStruct((b, t_all, GLA_W), F32)] * 2,
        grid=(b, nlb + ncb),
        in_specs=[spec(GLA_QK_W, fwd), spec(GLA_QK_W, fwd), spec(GLA_W, fwd), spec(GLA_QK_W, fwd, 0),
                  spec(GLA_QK_W, bwd), spec(GLA_QK_W, bwd), spec(GLA_W, bwd), spec(GLA_QK_W, bwd, 1),
                  cspec(tril), cspec(triu), cspec(ind), cspec(dmask)],
        out_specs=[spec(GLA_W, fwd), spec(GLA_W, bwd)],
        scratch_shapes=[pltpu.VMEM((GLA_DV, GLA_QK_W), F32), pltpu.VMEM((GLA_DV, GLA_QK_W), F32)],
        compiler_params=_cparams(("parallel", "arbitrary")),
        name="gla_scan",
    )(gq, gk, gv, lfb, gq, gk, gv, lfb, tril, triu, ind, dmask)


def _merge_kernel(x_ref, mod_ref, oml_ref, omc_ref, yfl_ref, yfc_ref, of_ref, ob_ref, og_ref, gate_ref,
                  gng_ref, wbm_ref, wbf_ref, wbg_ref, wo_ref, g2_ref, rw_ref, rb_ref,
                  x_out, h_out, idx_out, gw_out, *, n_lat_tiles):
    d = D_MODEL
    m = mod_ref[...]
    is_lat = pl.program_id(1) < n_lat_tiles
    y_fn = jnp.where(is_lat, yfl_ref[...], yfc_ref[...])
    o_mla = jnp.where(is_lat, oml_ref[...], omc_ref[...])
    o = of_ref[...] + ob_ref[...]
    gng = gng_ref[...]
    og = og_ref[...].astype(F32)
    parts = []
    for hd in range(GLA_HEADS):
        sl = slice(hd * GLA_DV, (hd + 1) * GLA_DV)
        parts.append(_rms(o[:, sl], gng[:, sl]))
    o_gla = (jnp.concatenate(parts, axis=1) * (og * jax.nn.sigmoid(og))).astype(BF16)

    gate = jax.nn.sigmoid(gate_ref[...].astype(F32))
    y = (gate[:, 0:d] * _dot(o_mla, wbm_ref[...])
         + gate[:, d:2 * d] * _dot(y_fn, wbf_ref[...])
         + gate[:, 2 * d:3 * d] * _dot(o_gla, wbg_ref[...]))
    x1 = x_ref[...] + m[:, 2 * d:3 * d] * _dot(y.astype(BF16), wo_ref[...])
    x_out[...] = x1
    h2 = _rms(x1, g2_ref[...]) * (1.0 + m[:, 4 * d:5 * d]) + m[:, 3 * d:4 * d]
    _store_row_tiles(h_out, _pack_bf16_pairs(h2))

    logits = _dot(h2.astype(BF16), rw_ref[...]) + rb_ref[...]
    lane = lax.broadcasted_iota(I32, logits.shape, 1)
    lane_f = lane.astype(F32)
    neg = jnp.float32(-jnp.inf)
    lg = jnp.where(lane < N_EXPERTS, logits, neg)
    idx_acc = jnp.zeros(logits.shape, I32)
    val_acc = jnp.zeros(logits.shape, F32)
    v0 = None
    for kk in range(TOP_K):
        mx = jnp.max(lg, axis=-1, keepdims=True)
        ix = jnp.min(jnp.where(lg == mx, lane_f, float(ROUTER_PAD)), axis=-1, keepdims=True).astype(I32)
        if kk == 0:
            v0 = mx
        idx_acc = jnp.where(lane == kk, ix, idx_acc)
        val_acc = jnp.where(lane == kk, jnp.exp(mx - v0), val_acc)
        lg = jnp.where(lane == ix, neg, lg)
    idx_out[...] = idx_acc
    gw_out[...] = val_acc / jnp.sum(val_acc, axis=-1, keepdims=True)


def _merge(xa, mod, o_mla_lat, o_mla_ctx, y_fn_lat, y_fn_ctx, o_f, o_b, og, gate, lw, tm, n_lat_tiles):
    b, t_all, d = xa.shape

    def tspec(w):
        return pl.BlockSpec((None, tm, w), lambda bi, i: (bi, i, 0))

    def lat_spec(w):
        return pl.BlockSpec((None, tm, w), lambda bi, i: (bi, jnp.minimum(i, n_lat_tiles - 1), 0))

    def ctx_spec(w):
        return pl.BlockSpec((None, tm, w), lambda bi, i: (bi, jnp.maximum(i - n_lat_tiles, 0), 0))

    def cspec(a):
        return pl.BlockSpec(a.shape, lambda bi, i: (0,) * a.ndim)

    consts = [lw["gla_norm_g"], lw["w_br_mla"], lw["w_br_fnet"], lw["w_br_gla"], lw["w_o"],
              lw["norm2_g"], lw["router_w"], lw["router_b"]]
    return pl.pallas_call(
        functools.partial(_merge_kernel, n_lat_tiles=n_lat_tiles),
        out_shape=[jax.ShapeDtypeStruct((b, t_all, d), F32), jax.ShapeDtypeStruct((b, t_all, ROW_TILES, 128), U32),
                   jax.ShapeDtypeStruct((b, t_all, ROUTER_PAD), I32),
                   jax.ShapeDtypeStruct((b, t_all, ROUTER_PAD), F32)],
        grid=(b, t_all // tm),
        in_specs=[
            tspec(d),
            pl.BlockSpec((None, None, 1, 6 * d), lambda bi, i: (bi, jnp.where(i < n_lat_tiles, 0, 1), 0, 0)),
            lat_spec(MLA_W), ctx_spec(MLA_W), lat_spec(FNET_W), ctx_spec(FNET_W),
            tspec(GLA_W), tspec(GLA_W), tspec(GLA_W), tspec(3 * d),
        ] + [cspec(a) for a in consts],
        out_specs=[tspec(d), pl.BlockSpec((None, tm, ROW_TILES, 128), lambda bi, i: (bi, i, 0, 0)),
                   tspec(ROUTER_PAD), tspec(ROUTER_PAD)],
        input_output_aliases={0: 0},
        compiler_params=_cparams(("parallel", "parallel")),
        name="merge_router",
    )(xa, mod, o_mla_lat, o_mla_ctx, y_fn_lat, y_fn_ctx, o_f, o_b, og, gate, *consts)


def _row_gather(idx_smem, slot, src_hbm, dst, sem, n_rows):
    for r in range(n_rows):
        pltpu.make_async_copy(src_hbm.at[pl.ds(idx_smem[slot, r], 1)], dst.at[slot, pl.ds(r, 1)],
                              sem.at[slot]).start(priority=r % 2)


def _gather_step(i, n, idx_hbm, src_hbm, idx_smem, buf, isem, gsem, n_rows):
    slot = lax.rem(i, 2)
    nxt = 1 - slot
    last = n - 1

    def idx_copy(blk, s):
        return pltpu.make_async_copy(idx_hbm.at[blk], idx_smem.at[s], isem.at[s])

    def rows_wait(s):
        pltpu.make_async_copy(buf.at[s], buf.at[s], gsem.at[s]).wait()

    @pl.when(i == 0)
    def _():
        idx_copy(0, 0).start()
        idx_copy(0, 0).wait()
        _row_gather(idx_smem, 0, src_hbm, buf, gsem, n_rows)
        idx_copy(jnp.minimum(1, last), 1).start()

    idx_copy(jnp.minimum(i + 1, last), nxt).wait()
    rows_wait(slot)
    _row_gather(idx_smem, nxt, src_hbm, buf, gsem, n_rows)
    idx_copy(jnp.minimum(i + 2, last), slot).start()

    def drain():
        @pl.when(i == last)
        def _():
            rows_wait(nxt)
            idx_copy(last, slot).wait()

    return slot, drain


def _moe_kernel(blk_e_ref, tok_hbm, h_hbm, wu_ref, bu_ref, wd_ref, bd_ref, y_out,
                idx_smem, xbuf, isem, gsem, wu_b, wd_b):
    i = pl.program_id(0)

    @pl.when((i == 0) | (blk_e_ref[i] != blk_e_ref[jnp.maximum(i - 1, 0)]))
    def _():
        wu_b[...] = wu_ref[...].astype(BF16)
        wd_b[...] = wd_ref[...].astype(BF16)

    slot, drain = _gather_step(i, pl.num_programs(0), tok_hbm, h_hbm, idx_smem, xbuf, isem, gsem, MOE_BLOCK)
    xb = jnp.concatenate(_unpack_bf16_pairs(_load_row_tiles(xbuf.at[slot])), axis=1).astype(BF16)
    up = _dot(xb, wu_b[...]) + bu_ref[...]
    glu = jnp.minimum(up[:, 0:D_EXPERT], SWIGLU_LIMIT)
    lin = jnp.clip(up[:, D_EXPERT:], -SWIGLU_LIMIT, SWIGLU_LIMIT)
    act = glu * jax.nn.sigmoid(SWIGLU_ALPHA * glu) * (lin + 1.0)
    yb = _dot(act.astype(BF16), wd_b[...]) + bd_ref[...]
    _store_row_tiles(y_out, _pack_bf16_pairs(yb))
    drain()


def _moe_experts(blk_e, row_tok, h_flat, w_up, b_up, w_down, b_down, layer):
    n_blocks = blk_e.shape[0]
    d = D_MODEL
    grid_spec = pltpu.PrefetchScalarGridSpec(
        num_scalar_prefetch=1,
        grid=(n_blocks,),
        in_specs=[
            pl.BlockSpec(memory_space=pl.ANY),
            pl.BlockSpec(memory_space=pl.ANY),
            pl.BlockSpec((None, None, d, 2 * D_EXPERT), lambda i, be: (layer, be[i], 0, 0)),
            pl.BlockSpec((None, None, 1, 2 * D_EXPERT), lambda i, be: (layer, be[i], 0, 0)),
            pl.BlockSpec((None, None, D_EXPERT, d), lambda i, be: (layer, be[i], 0, 0)),
            pl.BlockSpec((None, None, 1, d), lambda i, be: (layer, be[i], 0, 0)),
        ],
        out_specs=pl.BlockSpec((MOE_BLOCK, ROW_TILES, 128), lambda i, be: (i, 0, 0)),
        scratch_shapes=[pltpu.SMEM((2, MOE_BLOCK), I32), pltpu.VMEM((2, MOE_BLOCK, ROW_TILES, 128), U32),
                        pltpu.SemaphoreType.DMA((2,)), pltpu.SemaphoreType.DMA((2,)),
                        pltpu.VMEM((d, 2 * D_EXPERT), BF16), pltpu.VMEM((D_EXPERT, d), BF16)],
    )
    return pl.pallas_call(
        _moe_kernel,
        out_shape=jax.ShapeDtypeStruct((n_blocks * MOE_BLOCK, ROW_TILES, 128), U32),
        grid_spec=grid_spec,
        compiler_params=_cparams(("arbitrary",)),
        name="moe_experts",
    )(blk_e, row_tok, h_flat, w_up, b_up, w_down, b_down)


def _combine_kernel(pos_hbm, y_hbm, x_ref, mod_ref, gw_ref, x_out, idx_smem, ybuf, isem, gsem, *, tm, n_tiles):
    bi = pl.program_id(0)
    ti = pl.program_id(1)
    i = bi * n_tiles + ti
    n = pl.num_programs(0) * n_tiles
    slot, drain = _gather_step(i, n, pos_hbm, y_hbm, idx_smem, ybuf, isem, gsem, tm * TOP_K)
    gw = gw_ref[...]
    acc_lo = acc_hi = None
    for kk in range(TOP_K):
        lo, hi = _unpack_bf16_pairs(_load_row_tiles(ybuf.at[slot, pl.ds(kk * tm, tm)]))
        gk = gw[:, kk:kk + 1]
        acc_lo = lo * gk if kk == 0 else acc_lo + lo * gk
        acc_hi = hi * gk if kk == 0 else acc_hi + hi * gk
    d = D_MODEL
    x_out[...] = x_ref[...] + mod_ref[:, 5 * d:6 * d] * jnp.concatenate([acc_lo, acc_hi], axis=1)
    drain()


def _combine(xa, mod, gates, pos, y_sorted, tm, n_lat_tiles):
    b, t_all, d = xa.shape
    n_tiles = t_all // tm
    return pl.pallas_call(
        functools.partial(_combine_kernel, tm=tm, n_tiles=n_tiles),
        out_shape=jax.ShapeDtypeStruct((b, t_all, d), F32),
        grid=(b, n_tiles),
        in_specs=[
            pl.BlockSpec(memory_space=pl.ANY),
            pl.BlockSpec(memory_space=pl.ANY),
            pl.BlockSpec((None, tm, d), lambda bi, i: (bi, i, 0)),
            pl.BlockSpec((None, None, 1, 6 * d), lambda bi, i: (bi, jnp.where(i < n_lat_tiles, 0, 1), 0, 0)),
            pl.BlockSpec((None, tm, ROUTER_PAD), lambda bi, i: (bi, i, 0)),
        ],
        out_specs=pl.BlockSpec((None, tm, d), lambda bi, i: (bi, i, 0)),
        scratch_shapes=[pltpu.SMEM((2, tm * TOP_K), I32), pltpu.VMEM((2, tm * TOP_K, ROW_TILES, 128), U32),
                        pltpu.SemaphoreType.DMA((2,)), pltpu.SemaphoreType.DMA((2,))],
        input_output_aliases={2: 0},
        compiler_params=_cparams(("arbitrary", "arbitrary")),
        name="moe_combine",
    )(pos, y_sorted, xa, mod, gates)


def _final_kernel(x_ref, g_ref, o_ref):
    o_ref[...] = _rms(x_ref[...], g_ref[...])


def _final_norm(xa, g, seq, tm):
    b, _, d = xa.shape
    return pl.pallas_call(
        _final_kernel,
        out_shape=jax.ShapeDtypeStruct((b, seq, d), F32),
        grid=(b, seq // tm),
        in_specs=[pl.BlockSpec((None, tm, d), lambda bi, i: (bi, i, 0)),
                  pl.BlockSpec((1, d), lambda bi, i: (0, 0))],
        out_specs=pl.BlockSpec((None, tm, d), lambda bi, i: (bi, i, 0)),
        compiler_params=_cparams(("parallel", "parallel")),
        name="final_norm",
    )(xa, g.reshape(1, d))


def _pad_cols(w, width):
    return jnp.pad(w, ((0, 0), (0, width - w.shape[1])))


def _rope_partner(w):
    a = ROPE_AXIS // 2
    return jnp.concatenate([-w[:, a:2 * a], w[:, 0:a], -w[:, 3 * a:4 * a], w[:, 2 * a:3 * a]], axis=1)


def _layer_weights(l, w_in, mla_q_norm_g, mla_w_uq, mla_kv_norm_g, mla_w_ukv, gla_w_gate_f, gla_b_gate_f,
                   gla_w_gate_b, gla_b_gate_b, gla_norm_g, w_br_mla, w_br_fnet, w_br_gla, w_o, norm1_g,
                   norm2_g, router_w, router_b):
    d = D_MODEL
    splits = np.cumsum([MLA_Q_RANK, MLA_KV_RANK + MLA_ROPE, FNET_W, GLA_QK_W, GLA_QK_W, GLA_W, GLA_W,
                        GLA_GATE_RANK, GLA_GATE_RANK])
    wq, wkv, wfn, wgq, wgk, wgv, wog, wgf, wgb, wgate = jnp.split(w_in[l], [int(s) for s in splits], axis=1)
    w_in_p = jnp.concatenate([wq, _pad_cols(wkv, 256), wfn, wgq, wgk, wgv, wog,
                              _pad_cols(jnp.concatenate([wgf, wgb], axis=1), 128), wgate], axis=1).astype(BF16)

    qk = MLA_NOPE + MLA_ROPE
    zq = jnp.zeros((MLA_Q_RANK, HEAD_PAD - qk), F32)
    zn = jnp.zeros((MLA_Q_RANK, MLA_NOPE), F32)
    wqa, wqb = [], []
    for hd in range(MLA_HEADS):
        wh = mla_w_uq[l][:, hd * qk:(hd + 1) * qk]
        wqa.append(jnp.concatenate([wh, zq], axis=1))
        wqb.append(jnp.concatenate([zn, _rope_partner(wh[:, MLA_NOPE:]), zq], axis=1))
    eye = jnp.eye(MLA_ROPE, dtype=F32)
    wka, wkb, wv = [], [], []
    for hd in range(MLA_HEADS):
        wh = mla_w_ukv[l][:, hd * (MLA_NOPE + MLA_V):(hd + 1) * (MLA_NOPE + MLA_V)]
        top = jnp.concatenate([wh[:, 0:MLA_NOPE], jnp.zeros((MLA_KV_RANK, HEAD_PAD - MLA_NOPE), F32)], axis=1)
        mid_a = jnp.concatenate([jnp.zeros((MLA_ROPE, MLA_NOPE), F32), eye,
                                 jnp.zeros((MLA_ROPE, HEAD_PAD - qk), F32)], axis=1)
        mid_b = jnp.concatenate([jnp.zeros((MLA_ROPE, MLA_NOPE), F32), _rope_partner(eye),
                                 jnp.zeros((MLA_ROPE, HEAD_PAD - qk), F32)], axis=1)
        bot = jnp.zeros((256 - MLA_KV_RANK - MLA_ROPE, HEAD_PAD), F32)
        wka.append(jnp.concatenate([top, mid_a, bot], axis=0))
        wkb.append(jnp.concatenate([jnp.zeros_like(top), mid_b, bot], axis=0))
        wv.append(jnp.concatenate([wh[:, MLA_NOPE:], jnp.zeros((256 - MLA_KV_RANK, MLA_V), F32)], axis=0))

    ch = np.arange(FNET_GROUP_W)
    ang = 2.0 * np.pi * ((ch[:, None] * ch[None, :]) % FNET_GROUP_W) / FNET_GROUP_W
    wc = np.concatenate([np.cos(ang), np.sin(ang)], axis=1) / math.sqrt(FNET_GROUP_W)

    wg = jnp.zeros((128, 2 * GLA_QK_W), F32)
    wg = wg.at[0:GLA_GATE_RANK, 0:GLA_QK_W].set(gla_w_gate_f[l])
    wg = wg.at[GLA_GATE_RANK:2 * GLA_GATE_RANK, GLA_QK_W:].set(gla_w_gate_b[l])

    return {
        "norm1_g": norm1_g[l].reshape(1, d),
        "w_in": w_in_p,
        "q_norm_g": mla_q_norm_g[l].reshape(1, -1),
        "wqa": jnp.concatenate(wqa, axis=1).astype(BF16),
        "wqb": jnp.concatenate(wqb, axis=1).astype(BF16),
        "kv_norm_g": mla_kv_norm_g[l].reshape(1, -1),
        "wka": jnp.concatenate(wka, axis=1).astype(BF16),
        "wkb": jnp.concatenate(wkb, axis=1).astype(BF16),
        "wv": jnp.concatenate(wv, axis=1).astype(BF16),
        "wc": jnp.asarray(wc, BF16),
        "wg": wg.astype(BF16),
        "bg": jnp.concatenate([gla_b_gate_f[l], gla_b_gate_b[l]]).reshape(1, -1),
        "gla_norm_g": gla_norm_g[l].reshape(1, -1),
        "w_br_mla": w_br_mla[l].astype(BF16),
        "w_br_fnet": w_br_fnet[l].astype(BF16),
        "w_br_gla": w_br_gla[l].astype(BF16),
        "w_o": w_o[l].astype(BF16),
        "norm2_g": norm2_g[l].reshape(1, d),
        "router_w": _pad_cols(router_w[l], ROUTER_PAD).astype(BF16),
        "router_b": _pad_cols(router_b[l].reshape(1, -1), ROUTER_PAD),
    }


def _rope_tables(seq, ctx):
    rows = seq // GRID_W
    row = jnp.repeat(jnp.arange(rows, dtype=F32), GRID_W)
    col = jnp.tile(jnp.arange(GRID_W, dtype=F32), rows)
    inv_freq = ROPE_BASE ** (-jnp.arange(0, ROPE_AXIS, 2, dtype=F32) / ROPE_AXIS)
    ang_r = row[:, None] * inv_freq
    ang_c = col[:, None] * inv_freq
    ang = jnp.concatenate([ang_r, ang_r, ang_c, ang_c], axis=1)
    ang = jnp.concatenate([ang, jnp.zeros((ctx, MLA_ROPE), F32)], axis=0)
    t_all = seq + ctx
    pad = jnp.zeros((t_all, HEAD_PAD - MLA_NOPE - MLA_ROPE), F32)
    cos_t = jnp.concatenate([jnp.ones((t_all, MLA_NOPE), F32), jnp.cos(ang), pad], axis=1)
    sin_t = jnp.concatenate([jnp.zeros((t_all, MLA_NOPE), F32), jnp.sin(ang), pad], axis=1)
    return cos_t, sin_t


def _dft_mats(t):
    r = np.arange(t, dtype=np.int64)
    ang = 2.0 * np.pi * ((r[:, None] * r[None, :]) % t) / t
    s = 1.0 / math.sqrt(t)
    return jnp.asarray(np.cos(ang) * s, BF16), jnp.asarray(-np.sin(ang) * s, BF16)


def _routing(top_idx, n_tok):
    n_assign = n_tok * TOP_K
    flat_e = top_idx.reshape(n_assign)
    rt = 1024
    assert n_assign % rt == 0
    onehot = (flat_e[:, None] == jnp.arange(N_EXPERTS, dtype=I32)[None, :])
    oh3 = onehot.reshape(n_assign // rt, rt, N_EXPERTS).astype(BF16)
    tri = jnp.asarray(np.tril(np.ones((rt, rt), np.float32)), BF16)
    within = jnp.einsum("ts,nse->nte", tri, oh3, preferred_element_type=F32)
    tile_cnt = within[:, -1, :]
    tile_off = jnp.cumsum(tile_cnt, axis=0) - tile_cnt
    csum = (within + tile_off[:, None, :]).reshape(n_assign, N_EXPERTS)
    counts = (tile_off[-1] + tile_cnt[-1]).astype(I32)
    rank = jnp.sum(jnp.where(onehot, csum, 0.0), axis=1).astype(I32) - 1
    padded = ((counts + MOE_BLOCK - 1) // MOE_BLOCK) * MOE_BLOCK
    pad_end = jnp.cumsum(padded)
    pad_start = pad_end - padded
    dest = (pad_start[flat_e] + rank).astype(I32)
    n_blocks = -(-n_assign // MOE_BLOCK) + N_EXPERTS
    n_rows = n_blocks * MOE_BLOCK
    row_tok = jnp.zeros((n_rows,), I32).at[dest].set(jnp.arange(n_assign, dtype=I32) // TOP_K,
                                                     unique_indices=True)
    blk_start = jnp.arange(n_blocks, dtype=I32) * MOE_BLOCK
    blk_e = jnp.minimum(jnp.sum((pad_end[None, :] <= blk_start[:, None]).astype(I32), axis=1), N_EXPERTS - 1)
    return blk_e, row_tok.reshape(n_blocks, MOE_BLOCK), dest


def kernel(x, c, ctx, c_ctx, w_mod, b_mod, norm1_g, w_in, mla_q_norm_g, mla_w_uq, mla_kv_norm_g, mla_w_ukv,
           gla_w_gate_f, gla_b_gate_f, gla_w_gate_b, gla_b_gate_b, gla_norm_g, w_br_mla, w_br_fnet, w_br_gla,
           w_o, norm2_g, router_w, router_b, exp_w_up, exp_b_up, exp_w_down, exp_b_down, final_norm_g):
    b, seq, d = x.shape
    n_ctx = ctx.shape[1]
    n_layers = w_mod.shape[0]
    t_all = seq + n_ctx
    assert d == D_MODEL and seq % n_ctx == 0 and seq % GRID_W == 0 and n_ctx % GLA_CHUNK == 0
    tm = min(256, n_ctx)
    assert n_ctx % tm == 0 and seq % tm == 0 and (tm * TOP_K) % MOE_BLOCK == 0
    n_lat_tiles = seq // tm
    t_fourier = min(512, seq)
    t_attn = min(512, seq)

    xa = jnp.concatenate([x, ctx], axis=1)

    rows = -(-(b + 1) // 8) * 8
    cv = jnp.zeros((rows, d), F32).at[0:b].set(c).at[b].set(c_ctx)
    mod_all = _mod_all(cv, w_mod, b_mod)
    cos_t, sin_t = _rope_tables(seq, n_ctx)
    cm_l, sm_l = _dft_mats(seq)
    cm_c, sm_c = _dft_mats(n_ctx)

    n_tok = b * t_all
    b_up4 = exp_b_up.reshape(n_layers, N_EXPERTS, 1, 2 * D_EXPERT)
    b_down4 = exp_b_down.reshape(n_layers, N_EXPERTS, 1, d)
    for l in range(n_layers):
        lw = _layer_weights(l, w_in, mla_q_norm_g, mla_w_uq, mla_kv_norm_g, mla_w_ukv, gla_w_gate_f,
                            gla_b_gate_f, gla_w_gate_b, gla_b_gate_b, gla_norm_g, w_br_mla, w_br_fnet,
                            w_br_gla, w_o, norm1_g, norm2_g, router_w, router_b)
        mod_lat = mod_all[l, 0:b]
        mod_ctx = jnp.broadcast_to(mod_all[l, b][None, :], (b, 6 * d))
        mod = jnp.stack([mod_lat, mod_ctx], axis=1).reshape(b, 2, 1, 6 * d)

        qc, kc, v, pq, gq, gk, gv, lfb, og, gate = _inproj(xa, mod, cos_t, sin_t, lw, tm, n_lat_tiles)
        o_mla_lat = _attention(qc, kc, v, t_attn, 0, seq, 0, t_all)
        o_mla_ctx = _attention(qc, kc, v, tm, seq, n_ctx, seq, n_ctx)
        y_fn_lat = _fourier(pq, cm_l, sm_l, t_fourier, 0)
        y_fn_ctx = _fourier(pq, cm_c, sm_c, n_ctx, seq)
        o_f, o_b = _gla(gq, gk, gv, lfb, tm, seq, n_ctx)
        xa, h2, top_idx, gates = _merge(xa, mod, o_mla_lat, o_mla_ctx, y_fn_lat, y_fn_ctx, o_f, o_b, og, gate,
                                        lw, tm, n_lat_tiles)

        blk_e, row_tok, dest = _routing(top_idx[..., 0:TOP_K], n_tok)
        y_sorted = _moe_experts(blk_e, row_tok, h2.reshape(n_tok, ROW_TILES, 128), exp_w_up, b_up4, exp_w_down,
                                b_down4, l)
        pos = dest.reshape(n_tok // tm, tm, TOP_K).transpose(0, 2, 1).reshape(n_tok // tm, tm * TOP_K)
        xa = _combine(xa, mod, gates, pos, y_sorted, tm, n_lat_tiles)

    return _final_norm(xa, final_norm_g, seq, tm)
```

```python
import functools
import math

import jax
import jax.numpy as jnp
import numpy as np
from jax import lax
from jax.experimental import pallas as pl
from jax.experimental.pallas import tpu as pltpu

F32 = jnp.float32
BF16 = jnp.bfloat16
I32 = jnp.int32
U32 = jnp.uint32

D_MODEL = 1024
GRID_W = 64
EPS = 1e-6

MLA_HEADS = 8
MLA_Q_RANK = 256
MLA_KV_RANK = 128
MLA_NOPE = 64
MLA_ROPE = 32
MLA_V = 64
MLA_SCALE = (MLA_NOPE + MLA_ROPE) ** -0.5
ROPE_AXIS = MLA_ROPE // 2
ROPE_BASE = 10000.0
HEAD_PAD = 128

FNET_GROUPS = 4
FNET_GROUP_W = 128
FNET_W = FNET_GROUPS * FNET_GROUP_W

GLA_HEADS = 4
GLA_DK = 64
GLA_DV = 128
GLA_GATE_RANK = 16
GLA_TAU = 16.0
GLA_CHUNK = 64
GLA_SUB = 16
GLA_SCALE = GLA_DK ** -0.5
GLA_QK_W = GLA_HEADS * GLA_DK
GLA_W = GLA_HEADS * GLA_DV

N_EXPERTS = 32
TOP_K = 4
D_EXPERT = D_MODEL
SWIGLU_ALPHA = 1.702
SWIGLU_LIMIT = 7.0
MOE_BLOCK = 256
ROUTER_PAD = 128

MLA_W = MLA_HEADS * MLA_V

VMEM_LIMIT_V7X = 56 * 1024 * 1024

C_Q = 0
C_KV = C_Q + MLA_Q_RANK
C_FN = C_KV + 256
C_GQ = C_FN + FNET_W
C_GK = C_GQ + GLA_QK_W
C_GV = C_GK + GLA_QK_W
C_OG = C_GV + GLA_W
C_GG = C_OG + GLA_W
C_GATE = C_GG + 128
IN_WIDTH_PAD = C_GATE + 3 * D_MODEL


def _cparams(sem, vmem=VMEM_LIMIT_V7X):
    return pltpu.CompilerParams(dimension_semantics=sem, vmem_limit_bytes=vmem)


def _rms(x, g):
    return x * lax.rsqrt(jnp.mean(x * x, axis=-1, keepdims=True) + EPS) * g


def _dot(a, b):
    return jnp.dot(a, b, preferred_element_type=F32)


def _dot_nt(a, b):
    return lax.dot_general(a, b, (((1,), (1,)), ((), ())), preferred_element_type=F32)


def _dot_tn(a, b):
    return lax.dot_general(a, b, (((0,), (0,)), ((), ())), preferred_element_type=F32)


def _pack_bf16_pairs(x):
    w = x.shape[1] // 2
    r = lax.bitcast_convert_type(x.astype(BF16).astype(F32), U32)
    return r[:, w:] | (r[:, :w] >> 16)


def _unpack_bf16_pairs(u):
    lo = lax.bitcast_convert_type(u << 16, F32)
    hi = lax.bitcast_convert_type(u & jnp.uint32(0xFFFF0000), F32)
    return lo, hi


ROW_TILES = D_MODEL // 2 // 128


def _store_row_tiles(ref, x):
    for j in range(ROW_TILES):
        ref[:, j, :] = x[:, j * 128:(j + 1) * 128]


def _load_row_tiles(buf, slot, row0, m):
    return jnp.concatenate(
        [buf[slot, pl.ds(row0 * ROW_TILES + j, m, stride=ROW_TILES), :] for j in range(ROW_TILES)], axis=1)


def _mod_kernel(cv_ref, w_ref, b_ref, o_ref):
    cv = cv_ref[...]
    s = cv * jax.nn.sigmoid(cv)
    o_ref[...] = jnp.dot(s, w_ref[...], preferred_element_type=F32,
                         precision=lax.Precision.HIGHEST) + b_ref[...]


def _mod_all(cv, w_mod, b_mod):
    n_layers, d, w6 = w_mod.shape
    rows = cv.shape[0]
    tn = 1536
    return pl.pallas_call(
        _mod_kernel,
        out_shape=jax.ShapeDtypeStruct((n_layers, rows, w6), F32),
        grid=(n_layers, w6 // tn),
        in_specs=[
            pl.BlockSpec((rows, d), lambda l, j: (0, 0)),
            pl.BlockSpec((None, d, tn), lambda l, j: (l, 0, j)),
            pl.BlockSpec((None, 1, tn), lambda l, j: (l, 0, j)),
        ],
        out_specs=pl.BlockSpec((None, rows, tn), lambda l, j: (l, 0, j)),
        compiler_params=_cparams(("parallel", "parallel")),
        name="adaln_mod",
    )(cv, w_mod, b_mod.reshape(n_layers, 1, w6))


def _inproj_kernel(x_ref, mod_ref, cos_ref, sin_ref, g1_ref, win_ref, qng_ref, wqa_ref, wqb_ref,
                   kvg_ref, wka_ref, wkb_ref, wv_ref, wc_ref, wg_ref, bg_ref,
                   q_out, k_out, v_out, pq_out, gq_out, gk_out, gv_out, lfb_out, og_out, gate_out):
    d = D_MODEL
    x = x_ref[...]
    m = mod_ref[...]
    h = (_rms(x, g1_ref[...]) * (1.0 + m[:, d:2 * d]) + m[:, 0:d]).astype(BF16)

    def proj(a, b):
        return _dot(h, win_ref[:, a:b])

    cos = jnp.tile(cos_ref[...], (1, MLA_HEADS))
    sin = jnp.tile(sin_ref[...], (1, MLA_HEADS))

    nq = _rms(proj(C_Q, C_KV), qng_ref[...]).astype(BF16)
    q = (_dot(nq, wqa_ref[...]) * cos + _dot(nq, wqb_ref[...]) * sin) * MLA_SCALE
    q_out[...] = q.astype(BF16)
    ukv = proj(C_KV, C_FN)
    ckn = _rms(ukv[:, 0:MLA_KV_RANK], kvg_ref[...])
    lhs = jnp.concatenate([ckn, ukv[:, MLA_KV_RANK:]], axis=1).astype(BF16)
    k_out[...] = (_dot(lhs, wka_ref[...]) * cos + _dot(lhs, wkb_ref[...]) * sin).astype(BF16)
    v_out[...] = _dot(lhs, wv_ref[...]).astype(BF16)

    ufn = proj(C_FN, C_GQ).astype(BF16)
    ps, qs = [], []
    for g in range(FNET_GROUPS):
        r = _dot(ufn[:, g * FNET_GROUP_W:(g + 1) * FNET_GROUP_W], wc_ref[...])
        ps.append(r[:, 0:FNET_GROUP_W])
        qs.append(r[:, FNET_GROUP_W:])
    pq_out[...] = jnp.concatenate(ps + qs, axis=1).astype(BF16)

    gq_out[...] = (proj(C_GQ, C_GK) * GLA_SCALE).astype(BF16)
    gk_out[...] = proj(C_GK, C_GV).astype(BF16)
    gv_out[...] = proj(C_GV, C_OG).astype(BF16)
    og_out[...] = proj(C_OG, C_GG).astype(BF16)
    z = _dot(proj(C_GG, C_GATE).astype(BF16), wg_ref[...]) + bg_ref[...]
    lfb_out[...] = (jnp.minimum(z, 0.0) - jnp.log1p(jnp.exp(-jnp.abs(z)))) * (1.0 / GLA_TAU)
    gate_out[...] = proj(C_GATE, IN_WIDTH_PAD).astype(BF16)


def _inproj(xa, mod, cos_t, sin_t, lw, tm, n_lat_tiles):
    b, t_all, d = xa.shape
    nt = t_all // tm

    def tok(w, dt):
        return jax.ShapeDtypeStruct((b, t_all, w), dt)

    def tspec(w):
        return pl.BlockSpec((None, tm, w), lambda bi, i: (bi, i, 0))

    def cspec(a):
        return pl.BlockSpec(a.shape, lambda bi, i: (0,) * a.ndim)

    consts = [lw["norm1_g"], lw["w_in"], lw["q_norm_g"], lw["wqa"], lw["wqb"], lw["kv_norm_g"],
              lw["wka"], lw["wkb"], lw["wv"], lw["wc"], lw["wg"], lw["bg"]]
    widths = [(1024, BF16), (1024, BF16), (MLA_W, BF16), (2 * FNET_W, BF16), (GLA_QK_W, BF16),
              (GLA_QK_W, BF16), (GLA_W, BF16), (2 * GLA_QK_W, F32), (GLA_W, BF16), (3 * D_MODEL, BF16)]
    return pl.pallas_call(
        _inproj_kernel,
        out_shape=[tok(w, dt) for w, dt in widths],
        grid=(b, nt),
        in_specs=[
            tspec(d),
            pl.BlockSpec((None, None, 1, 6 * d), lambda bi, i: (bi, jnp.where(i < n_lat_tiles, 0, 1), 0, 0)),
            pl.BlockSpec((tm, HEAD_PAD), lambda bi, i: (i, 0)),
            pl.BlockSpec((tm, HEAD_PAD), lambda bi, i: (i, 0)),
        ] + [cspec(a) for a in consts],
        out_specs=[tspec(w) for w, _ in widths],
        compiler_params=_cparams(("parallel", "parallel")),
        name="inproj",
    )(xa, mod, cos_t, sin_t, *consts)


def _attn_kernel(q_ref, k_ref, v_ref, o_ref):
    lane = lax.broadcasted_iota(I32, (q_ref.shape[0], 2 * MLA_V), 1)
    outs = []
    for hp in range(MLA_HEADS // 2):
        vp = v_ref[:, hp * 2 * MLA_V:(hp + 1) * 2 * MLA_V]
        rs = []
        for j in range(2):
            hd = 2 * hp + j
            q = q_ref[:, hd * HEAD_PAD:(hd + 1) * HEAD_PAD]
            k = k_ref[:, hd * HEAD_PAD:(hd + 1) * HEAD_PAD]
            s = _dot_nt(q, k)
            p = jnp.exp(s - jnp.max(s, axis=-1, keepdims=True))
            l = jnp.sum(p, axis=-1, keepdims=True)
            rs.append(_dot(p.astype(BF16), vp) / l)
        outs.append(jnp.where(lane < MLA_V, rs[0], rs[1]))
    o_ref[...] = jnp.concatenate(outs, axis=1).astype(BF16)


def _attention(qc, kc, v, tq, q_row0, n_q, k_row0, n_k):
    b = qc.shape[0]
    assert q_row0 % tq == 0 and n_q % tq == 0 and k_row0 % n_k == 0
    return pl.pallas_call(
        _attn_kernel,
        out_shape=jax.ShapeDtypeStruct((b, n_q, MLA_W), BF16),
        grid=(b, n_q // tq),
        in_specs=[
            pl.BlockSpec((None, tq, MLA_HEADS * HEAD_PAD), lambda bi, i: (bi, q_row0 // tq + i, 0)),
            pl.BlockSpec((None, n_k, MLA_HEADS * HEAD_PAD), lambda bi, i: (bi, k_row0 // n_k, 0),
                         pipeline_mode=pl.Buffered(1)),
            pl.BlockSpec((None, n_k, MLA_W), lambda bi, i: (bi, k_row0 // n_k, 0),
                         pipeline_mode=pl.Buffered(1)),
        ],
        out_specs=pl.BlockSpec((None, tq, MLA_W), lambda bi, i: (bi, i, 0)),
        compiler_params=_cparams(("parallel", "parallel")),
        name="mla_attention",
    )(qc, kc, v)


def _fourier_kernel(cm_ref, sm_ref, pq_ref, o_ref):
    y = _dot(cm_ref[...], pq_ref[:, 0:FNET_W]) + _dot(sm_ref[...], pq_ref[:, FNET_W:])
    o_ref[...] = y.astype(BF16)


def _fourier(pq, cm, sm, tm, row0):
    b = pq.shape[0]
    t = cm.shape[0]
    return pl.pallas_call(
        _fourier_kernel,
        out_shape=jax.ShapeDtypeStruct((b, t, FNET_W), BF16),
        grid=(t // tm, b),
        in_specs=[
            pl.BlockSpec((tm, t), lambda i, bi: (i, 0)),
            pl.BlockSpec((tm, t), lambda i, bi: (i, 0)),
            pl.BlockSpec((None, t, 2 * FNET_W), lambda i, bi: (bi, row0 // t, 0)),
        ],
        out_specs=pl.BlockSpec((None, tm, FNET_W), lambda i, bi: (bi, i, 0)),
        compiler_params=_cparams(("parallel", "parallel")),
        name="fourier_mix",
    )(cm, sm, pq)


def _gla_chunk(q, k, v, f, tri, ind_ref, dmask_ref, st_ref, reverse):
    c = GLA_CHUNK
    f0 = f.astype(BF16)
    r1 = f - f0.astype(F32)
    f1 = r1.astype(BF16)
    f2 = (r1 - f1.astype(F32)).astype(BF16)
    g = _dot(tri, f0) + _dot(tri, f1) + _dot(tri, f2)
    e = g[0:1, :] if reverse else g[c - 1:c, :]
    qf = q.astype(F32)
    kf = k.astype(F32)
    qg = (qf * jnp.exp(g)).astype(BF16)
    kd = (kf * jnp.exp(e - g)).astype(BF16)
    st = st_ref[...]
    st_b = st.astype(BF16)

    sb = GLA_SUB
    ind = ind_ref[...]
    row_sb = lax.broadcasted_iota(I32, (sb, GLA_QK_W), 0)
    row_c = lax.broadcasted_iota(I32, (c, GLA_QK_W), 0)
    a_rows = []
    for i in range(c // sb):
        lo = i * sb
        gb, kb, qb = g[lo:lo + sb], kf[lo:lo + sb], qf[lo:lo + sb]
        ws = []
        for t in range(sb):
            w = jnp.exp(jnp.minimum(gb[t:t + 1] - gb, 0.0)) * kb * qb[t:t + 1]
            keep = (row_sb >= t) if reverse else (row_sb <= t)
            ws.append(jnp.where(keep, w, 0.0))
        w_all = jnp.concatenate(ws, axis=0).astype(BF16)
        r = _dot(w_all, ind) * dmask_ref[i]
        a_i = jnp.sum(r.reshape(sb, sb, GLA_QK_W), axis=1)
        ref_row = lo + sb if reverse else lo - 1
        if 0 <= ref_row < c:
            gr = g[ref_row:ref_row + 1]
            qt = (qb * jnp.exp(gb - gr)).astype(BF16)
            valid = (row_c >= lo + sb) if reverse else (row_c < lo)
            kt = jnp.where(valid, kf * jnp.exp(jnp.minimum(gr - g, 0.0)), 0.0).astype(BF16)
            kbd = jnp.tile(kt, (GLA_HEADS, 1)) * ind
            a_i = a_i + _dot_nt(qt, kbd)
        a_rows.append(a_i)
    a = jnp.concatenate(a_rows, axis=0).astype(BF16)

    outs, upd = [], []
    for hd in range(GLA_HEADS):
        ks_ = slice(hd * GLA_DK, (hd + 1) * GLA_DK)
        vh = v[:, hd * GLA_DV:(hd + 1) * GLA_DV]
        o_inter = _dot_nt(qg[:, ks_], st_b[:, ks_])
        o_intra = _dot(a[:, ks_], vh)
        outs.append(o_inter + o_intra)
        upd.append(_dot_tn(vh, kd[:, ks_]))
    st_ref[...] = st * jnp.exp(e) + jnp.concatenate(upd, axis=1)
    return jnp.concatenate(outs, axis=1)


def _gla_kernel(qf_ref, kf_ref, vf_ref, ff_ref, qb_ref, kb_ref, vb_ref, fb_ref,
                tril_ref, triu_ref, ind_ref, dmask_ref, of_ref, ob_ref,
                stf, stb, *, n_chunks):
    @pl.when(pl.program_id(1) == 0)
    def _():
        stf[...] = jnp.zeros_like(stf)
        stb[...] = jnp.zeros_like(stb)

    c = GLA_CHUNK
    for j in range(n_chunks):
        rows = slice(j * c, (j + 1) * c)
        of_ref[rows, :] = _gla_chunk(qf_ref[rows, :], kf_ref[rows, :], vf_ref[rows, :], ff_ref[rows, :],
                                     tril_ref[...], ind_ref, dmask_ref, stf, False)
    for j in reversed(range(n_chunks)):
        rows = slice(j * c, (j + 1) * c)
        ob_ref[rows, :] = _gla_chunk(qb_ref[rows, :], kb_ref[rows, :], vb_ref[rows, :], fb_ref[rows, :],
                                     triu_ref[...], ind_ref, dmask_ref, stb, True)


def _gla(gq, gk, gv, lfb, tb, seq, ctx):
    b, t_all, _ = gq.shape
    nlb, ncb = seq // tb, ctx // tb
    c = GLA_CHUNK

    def fwd(i):
        return jnp.where(i < ncb, nlb + i, i - ncb)

    def bwd(i):
        return jnp.where(i < ncb, nlb + ncb - 1 - i, nlb - 1 - (i - ncb))

    def spec(w, order, col=0):
        return pl.BlockSpec((None, tb, w), lambda bi, i: (bi, order(i), col))

    r = np.arange(c)
    tril = jnp.asarray(r[:, None] >= r[None, :], BF16)
    triu = jnp.asarray(r[:, None] <= r[None, :], BF16)
    hs = np.arange(GLA_QK_W)
    ind = jnp.asarray(hs[:, None] // GLA_DK == hs[None, :] // GLA_DK, BF16)
    sb = GLA_SUB
    s_loc = np.tile(np.arange(sb), sb)
    dmask = jnp.asarray(np.stack([(i * sb + s_loc)[:, None] == (hs % GLA_DK)[None, :] for i in range(c // sb)]),
                        F32)

    def cspec(a):
        return pl.BlockSpec(a.shape, lambda bi, i: (0,) * a.ndim)

    return pl.pallas_call(
        functools.partial(_gla_kernel, n_chunks=tb // c),
        out_shape=[jax.ShapeDtypeStruct((b, t_all, GLA_W), F32)] * 2,
        grid=(b, nlb + ncb),
        in_specs=[spec(GLA_QK_W, fwd), spec(GLA_QK_W, fwd), spec(GLA_W, fwd), spec(GLA_QK_W, fwd, 0),
                  spec(GLA_QK_W, bwd), spec(GLA_QK_W, bwd), spec(GLA_W, bwd), spec(GLA_QK_W, bwd, 1),
                  cspec(tril), cspec(triu), cspec(ind), cspec(dmask)],
        out_specs=[spec(GLA_W, fwd), spec(GLA_W, bwd)],
        scratch_shapes=[pltpu.VMEM((GLA_DV, GLA_QK_W), F32), pltpu.VMEM((GLA_DV, GLA_QK_W), F32)],
        compiler_params=_cparams(("parallel", "arbitrary")),
        name="gla_scan",
    )(gq, gk, gv, lfb, gq, gk, gv, lfb, tril, triu, ind, dmask)


def _merge_kernel(x_ref, mod_ref, oml_ref, omc_ref, yfl_ref, yfc_ref, of_ref, ob_ref, og_ref, gate_ref,
                  gng_ref, wbm_ref, wbf_ref, wbg_ref, wo_ref, g2_ref, rw_ref, rb_ref,
                  x_out, h_out, idx_out, gw_out, *, n_lat_tiles):
    d = D_MODEL
    m = mod_ref[...]
    is_lat = pl.program_id(1) < n_lat_tiles
    y_fn = jnp.where(is_lat, yfl_ref[...], yfc_ref[...])
    o_mla = jnp.where(is_lat, oml_ref[...], omc_ref[...])
    o = of_ref[...] + ob_ref[...]
    gng = gng_ref[...]
    og = og_ref[...].astype(F32)
    parts = []
    for hd in range(GLA_HEADS):
        sl = slice(hd * GLA_DV, (hd + 1) * GLA_DV)
        parts.append(_rms(o[:, sl], gng[:, sl]))
    o_gla = (jnp.concatenate(parts, axis=1) * (og * jax.nn.sigmoid(og))).astype(BF16)

    gate = jax.nn.sigmoid(gate_ref[...].astype(F32))
    y = (gate[:, 0:d] * _dot(o_mla, wbm_ref[...])
         + gate[:, d:2 * d] * _dot(y_fn, wbf_ref[...])
         + gate[:, 2 * d:3 * d] * _dot(o_gla, wbg_ref[...]))
    x1 = x_ref[...] + m[:, 2 * d:3 * d] * _dot(y.astype(BF16), wo_ref[...])
    x_out[...] = x1
    h2 = _rms(x1, g2_ref[...]) * (1.0 + m[:, 4 * d:5 * d]) + m[:, 3 * d:4 * d]
    _store_row_tiles(h_out, _pack_bf16_pairs(h2))

    logits = _dot(h2.astype(BF16), rw_ref[...]) + rb_ref[...]
    lane = lax.broadcasted_iota(I32, logits.shape, 1)
    lane_f = lane.astype(F32)
    neg = jnp.float32(-jnp.inf)
    lg = jnp.where(lane < N_EXPERTS, logits, neg)
    idx_acc = jnp.zeros(logits.shape, I32)
    val_acc = jnp.zeros(logits.shape, F32)
    v0 = None
    for kk in range(TOP_K):
        mx = jnp.max(lg, axis=-1, keepdims=True)
        ix = jnp.min(jnp.where(lg == mx, lane_f, float(ROUTER_PAD)), axis=-1, keepdims=True).astype(I32)
        if kk == 0:
            v0 = mx
        idx_acc = jnp.where(lane == kk, ix, idx_acc)
        val_acc = jnp.where(lane == kk, jnp.exp(mx - v0), val_acc)
        lg = jnp.where(lane == ix, neg, lg)
    idx_out[...] = idx_acc
    gw_out[...] = val_acc / jnp.sum(val_acc, axis=-1, keepdims=True)


def _merge(xa, mod, o_mla_lat, o_mla_ctx, y_fn_lat, y_fn_ctx, o_f, o_b, og, gate, lw, tm, n_lat_tiles):
    b, t_all, d = xa.shape

    def tspec(w):
        return pl.BlockSpec((None, tm, w), lambda bi, i: (bi, i, 0))

    def lat_spec(w):
        return pl.BlockSpec((None, tm, w), lambda bi, i: (bi, jnp.minimum(i, n_lat_tiles - 1), 0))

    def ctx_spec(w):
        return pl.BlockSpec((None, tm, w), lambda bi, i: (bi, jnp.maximum(i - n_lat_tiles, 0), 0))

    def cspec(a):
        return pl.BlockSpec(a.shape, lambda bi, i: (0,) * a.ndim)

    consts = [lw["gla_norm_g"], lw["w_br_mla"], lw["w_br_fnet"], lw["w_br_gla"], lw["w_o"],
              lw["norm2_g"], lw["router_w"], lw["router_b"]]
    return pl.pallas_call(
        functools.partial(_merge_kernel, n_lat_tiles=n_lat_tiles),
        out_shape=[jax.ShapeDtypeStruct((b, t_all, d), F32), jax.ShapeDtypeStruct((b, t_all, ROW_TILES, 128), U32),
                   jax.ShapeDtypeStruct((b, t_all, ROUTER_PAD), I32),
                   jax.ShapeDtypeStruct((b, t_all, ROUTER_PAD), F32)],
        grid=(b, t_all // tm),
        in_specs=[
            tspec(d),
            pl.BlockSpec((None, None, 1, 6 * d), lambda bi, i: (bi, jnp.where(i < n_lat_tiles, 0, 1), 0, 0)),
            lat_spec(MLA_W), ctx_spec(MLA_W), lat_spec(FNET_W), ctx_spec(FNET_W),
            tspec(GLA_W), tspec(GLA_W), tspec(GLA_W), tspec(3 * d),
        ] + [cspec(a) for a in consts],
        out_specs=[tspec(d), pl.BlockSpec((None, tm, ROW_TILES, 128), lambda bi, i: (bi, i, 0, 0)),
                   tspec(ROUTER_PAD), tspec(ROUTER_PAD)],
        input_output_aliases={0: 0},
        compiler_params=_cparams(("parallel", "parallel")),
        name="merge_router",
    )(xa, mod, o_mla_lat, o_mla_ctx, y_fn_lat, y_fn_ctx, o_f, o_b, og, gate, *consts)


def _row_gather(idx_smem, slot, src_hbm, dst, sem, n_rows):
    for r in range(n_rows):
        pltpu.make_async_copy(src_hbm.at[idx_smem[slot, r]], dst.at[slot, pl.ds(r * ROW_TILES, ROW_TILES)],
                              sem.at[slot]).start(priority=r % 2)


def _gather_step(i, n, idx_hbm, src_hbm, idx_smem, buf, isem, gsem, n_rows):
    slot = lax.rem(i, 2)
    nxt = 1 - slot
    last = n - 1

    def idx_copy(blk, s):
        return pltpu.make_async_copy(idx_hbm.at[blk], idx_smem.at[s], isem.at[s])

    def rows_wait(s):
        pltpu.make_async_copy(buf.at[s], buf.at[s], gsem.at[s]).wait()

    @pl.when(i == 0)
    def _():
        idx_copy(0, 0).start()
        idx_copy(0, 0).wait()
        _row_gather(idx_smem, 0, src_hbm, buf, gsem, n_rows)
        idx_copy(jnp.minimum(1, last), 1).start()

    idx_copy(jnp.minimum(i + 1, last), nxt).wait()
    rows_wait(slot)
    _row_gather(idx_smem, nxt, src_hbm, buf, gsem, n_rows)
    idx_copy(jnp.minimum(i + 2, last), slot).start()

    def drain():
        @pl.when(i == last)
        def _():
            rows_wait(nxt)
            idx_copy(last, slot).wait()

    return slot, drain


def _moe_kernel(blk_e_ref, tok_hbm, h_hbm, wu_ref, bu_ref, wd_ref, bd_ref, y_out,
                idx_smem, xbuf, isem, gsem, wu_b, wd_b):
    i = pl.program_id(0)

    @pl.when((i == 0) | (blk_e_ref[i] != blk_e_ref[jnp.maximum(i - 1, 0)]))
    def _():
        wu_b[...] = wu_ref[...].astype(BF16)
        wd_b[...] = wd_ref[...].astype(BF16)

    slot, drain = _gather_step(i, pl.num_programs(0), tok_hbm, h_hbm, idx_smem, xbuf, isem, gsem, MOE_BLOCK)
    xb = jnp.concatenate(_unpack_bf16_pairs(_load_row_tiles(xbuf, slot, 0, MOE_BLOCK)), axis=1).astype(BF16)
    up = _dot(xb, wu_b[...]) + bu_ref[...]
    glu = jnp.minimum(up[:, 0:D_EXPERT], SWIGLU_LIMIT)
    lin = jnp.clip(up[:, D_EXPERT:], -SWIGLU_LIMIT, SWIGLU_LIMIT)
    act = glu * jax.nn.sigmoid(SWIGLU_ALPHA * glu) * (lin + 1.0)
    yb = _dot(act.astype(BF16), wd_b[...]) + bd_ref[...]
    _store_row_tiles(y_out, _pack_bf16_pairs(yb))
    drain()


def _moe_experts(blk_e, row_tok, h_flat, w_up, b_up, w_down, b_down, layer):
    n_blocks = blk_e.shape[0]
    d = D_MODEL
    grid_spec = pltpu.PrefetchScalarGridSpec(
        num_scalar_prefetch=1,
        grid=(n_blocks,),
        in_specs=[
            pl.BlockSpec(memory_space=pl.ANY),
            pl.BlockSpec(memory_space=pl.ANY),
            pl.BlockSpec((None, None, d, 2 * D_EXPERT), lambda i, be: (layer, be[i], 0, 0)),
            pl.BlockSpec((None, None, 1, 2 * D_EXPERT), lambda i, be: (layer, be[i], 0, 0)),
            pl.BlockSpec((None, None, D_EXPERT, d), lambda i, be: (layer, be[i], 0, 0)),
            pl.BlockSpec((None, None, 1, d), lambda i, be: (layer, be[i], 0, 0)),
        ],
        out_specs=pl.BlockSpec((MOE_BLOCK, ROW_TILES, 128), lambda i, be: (i, 0, 0)),
        scratch_shapes=[pltpu.SMEM((2, MOE_BLOCK), I32), pltpu.VMEM((2, MOE_BLOCK * ROW_TILES, 128), U32),
                        pltpu.SemaphoreType.DMA((2,)), pltpu.SemaphoreType.DMA((2,)),
                        pltpu.VMEM((d, 2 * D_EXPERT), BF16), pltpu.VMEM((D_EXPERT, d), BF16)],
    )
    return pl.pallas_call(
        _moe_kernel,
        out_shape=jax.ShapeDtypeStruct((n_blocks * MOE_BLOCK, ROW_TILES, 128), U32),
        grid_spec=grid_spec,
        compiler_params=_cparams(("arbitrary",)),
        name="moe_experts",
    )(blk_e, row_tok, h_flat, w_up, b_up, w_down, b_down)


def _combine_kernel(pos_hbm, y_hbm, x_ref, mod_ref, gw_ref, x_out, idx_smem, ybuf, isem, gsem, *, tm, n_tiles):
    bi = pl.program_id(0)
    ti = pl.program_id(1)
    i = bi * n_tiles + ti
    n = pl.num_programs(0) * n_tiles
    slot, drain = _gather_step(i, n, pos_hbm, y_hbm, idx_smem, ybuf, isem, gsem, tm * TOP_K)
    gw = gw_ref[...]
    acc_lo = acc_hi = None
    for kk in range(TOP_K):
        lo, hi = _unpack_bf16_pairs(_load_row_tiles(ybuf, slot, kk * tm, tm))
        gk = gw[:, kk:kk + 1]
        acc_lo = lo * gk if kk == 0 else acc_lo + lo * gk
        acc_hi = hi * gk if kk == 0 else acc_hi + hi * gk
    d = D_MODEL
    x_out[...] = x_ref[...] + mod_ref[:, 5 * d:6 * d] * jnp.concatenate([acc_lo, acc_hi], axis=1)
    drain()


def _combine(xa, mod, gates, pos, y_sorted, tm, n_lat_tiles):
    b, t_all, d = xa.shape
    n_tiles = t_all // tm
    return pl.pallas_call(
        functools.partial(_combine_kernel, tm=tm, n_tiles=n_tiles),
        out_shape=jax.ShapeDtypeStruct((b, t_all, d), F32),
        grid=(b, n_tiles),
        in_specs=[
            pl.BlockSpec(memory_space=pl.ANY),
            pl.BlockSpec(memory_space=pl.ANY),
            pl.BlockSpec((None, tm, d), lambda bi, i: (bi, i, 0)),
            pl.BlockSpec((None, None, 1, 6 * d), lambda bi, i: (bi, jnp.where(i < n_lat_tiles, 0, 1), 0, 0)),
            pl.BlockSpec((None, tm, ROUTER_PAD), lambda bi, i: (bi, i, 0)),
        ],
        out_specs=pl.BlockSpec((None, tm, d), lambda bi, i: (bi, i, 0)),
        scratch_shapes=[pltpu.SMEM((2, tm * TOP_K), I32), pltpu.VMEM((2, tm * TOP_K * ROW_TILES, 128), U32),
                        pltpu.SemaphoreType.DMA((2,)), pltpu.SemaphoreType.DMA((2,))],
        input_output_aliases={2: 0},
        compiler_params=_cparams(("arbitrary", "arbitrary")),
        name="moe_combine",
    )(pos, y_sorted, xa, mod, gates)


def _final_kernel(x_ref, g_ref, o_ref):
    o_ref[...] = _rms(x_ref[...], g_ref[...])


def _final_norm(xa, g, seq, tm):
    b, _, d = xa.shape
    return pl.pallas_call(
        _final_kernel,
        out_shape=jax.ShapeDtypeStruct((b, seq, d), F32),
        grid=(b, seq // tm),
        in_specs=[pl.BlockSpec((None, tm, d), lambda bi, i: (bi, i, 0)),
                  pl.BlockSpec((1, d), lambda bi, i: (0, 0))],
        out_specs=pl.BlockSpec((None, tm, d), lambda bi, i: (bi, i, 0)),
        compiler_params=_cparams(("parallel", "parallel")),
        name="final_norm",
    )(xa, g.reshape(1, d))


def _pad_cols(w, width):
    return jnp.pad(w, ((0, 0), (0, width - w.shape[1])))


def _rope_partner(w):
    a = ROPE_AXIS // 2
    return jnp.concatenate([-w[:, a:2 * a], w[:, 0:a], -w[:, 3 * a:4 * a], w[:, 2 * a:3 * a]], axis=1)


def _layer_weights(l, w_in, mla_q_norm_g, mla_w_uq, mla_kv_norm_g, mla_w_ukv, gla_w_gate_f, gla_b_gate_f,
                   gla_w_gate_b, gla_b_gate_b, gla_norm_g, w_br_mla, w_br_fnet, w_br_gla, w_o, norm1_g,
                   norm2_g, router_w, router_b):
    d = D_MODEL
    splits = np.cumsum([MLA_Q_RANK, MLA_KV_RANK + MLA_ROPE, FNET_W, GLA_QK_W, GLA_QK_W, GLA_W, GLA_W,
                        GLA_GATE_RANK, GLA_GATE_RANK])
    wq, wkv, wfn, wgq, wgk, wgv, wog, wgf, wgb, wgate = jnp.split(w_in[l], [int(s) for s in splits], axis=1)
    w_in_p = jnp.concatenate([wq, _pad_cols(wkv, 256), wfn, wgq, wgk, wgv, wog,
                              _pad_cols(jnp.concatenate([wgf, wgb], axis=1), 128), wgate], axis=1).astype(BF16)

    qk = MLA_NOPE + MLA_ROPE
    zq = jnp.zeros((MLA_Q_RANK, HEAD_PAD - qk), F32)
    zn = jnp.zeros((MLA_Q_RANK, MLA_NOPE), F32)
    wqa, wqb = [], []
    for hd in range(MLA_HEADS):
        wh = mla_w_uq[l][:, hd * qk:(hd + 1) * qk]
        wqa.append(jnp.concatenate([wh, zq], axis=1))
        wqb.append(jnp.concatenate([zn, _rope_partner(wh[:, MLA_NOPE:]), zq], axis=1))
    eye = jnp.eye(MLA_ROPE, dtype=F32)
    wka, wkb, wv = [], [], []
    for hd in range(MLA_HEADS):
        wh = mla_w_ukv[l][:, hd * (MLA_NOPE + MLA_V):(hd + 1) * (MLA_NOPE + MLA_V)]
        top = jnp.concatenate([wh[:, 0:MLA_NOPE], jnp.zeros((MLA_KV_RANK, HEAD_PAD - MLA_NOPE), F32)], axis=1)
        mid_a = jnp.concatenate([jnp.zeros((MLA_ROPE, MLA_NOPE), F32), eye,
                                 jnp.zeros((MLA_ROPE, HEAD_PAD - qk), F32)], axis=1)
        mid_b = jnp.concatenate([jnp.zeros((MLA_ROPE, MLA_NOPE), F32), _rope_partner(eye),
                                 jnp.zeros((MLA_ROPE, HEAD_PAD - qk), F32)], axis=1)
        bot = jnp.zeros((256 - MLA_KV_RANK - MLA_ROPE, HEAD_PAD), F32)
        wka.append(jnp.concatenate([top, mid_a, bot], axis=0))
        wkb.append(jnp.concatenate([jnp.zeros_like(top), mid_b, bot], axis=0))
        wv.append(jnp.concatenate([wh[:, MLA_NOPE:], jnp.zeros((256 - MLA_KV_RANK, MLA_V), F32)], axis=0))

    ch = np.arange(FNET_GROUP_W)
    ang = 2.0 * np.pi * ((ch[:, None] * ch[None, :]) % FNET_GROUP_W) / FNET_GROUP_W
    wc = np.concatenate([np.cos(ang), np.sin(ang)], axis=1) / math.sqrt(FNET_GROUP_W)

    wg = jnp.zeros((128, 2 * GLA_QK_W), F32)
    wg = wg.at[0:GLA_GATE_RANK, 0:GLA_QK_W].set(gla_w_gate_f[l])
    wg = wg.at[GLA_GATE_RANK:2 * GLA_GATE_RANK, GLA_QK_W:].set(gla_w_gate_b[l])

    return {
        "norm1_g": norm1_g[l].reshape(1, d),
        "w_in": w_in_p,
        "q_norm_g": mla_q_norm_g[l].reshape(1, -1),
        "wqa": jnp.concatenate(wqa, axis=1).astype(BF16),
        "wqb": jnp.concatenate(wqb, axis=1).astype(BF16),
        "kv_norm_g": mla_kv_norm_g[l].reshape(1, -1),
        "wka": jnp.concatenate(wka, axis=1).astype(BF16),
        "wkb": jnp.concatenate(wkb, axis=1).astype(BF16),
        "wv": jnp.concatenate(wv, axis=1).astype(BF16),
        "wc": jnp.asarray(wc, BF16),
        "wg": wg.astype(BF16),
        "bg": jnp.concatenate([gla_b_gate_f[l], gla_b_gate_b[l]]).reshape(1, -1),
        "gla_norm_g": gla_norm_g[l].reshape(1, -1),
        "w_br_mla": w_br_mla[l].astype(BF16),
        "w_br_fnet": w_br_fnet[l].astype(BF16),
        "w_br_gla": w_br_gla[l].astype(BF16),
        "w_o": w_o[l].astype(BF16),
        "norm2_g": norm2_g[l].reshape(1, d),
        "router_w": _pad_cols(router_w[l], ROUTER_PAD).astype(BF16),
        "router_b": _pad_cols(router_b[l].reshape(1, -1), ROUTER_PAD),
    }


def _rope_tables(seq, ctx):
    rows = seq // GRID_W
    row = jnp.repeat(jnp.arange(rows, dtype=F32), GRID_W)
    col = jnp.tile(jnp.arange(GRID_W, dtype=F32), rows)
    inv_freq = ROPE_BASE ** (-jnp.arange(0, ROPE_AXIS, 2, dtype=F32) / ROPE_AXIS)
    ang_r = row[:, None] * inv_freq
    ang_c = col[:, None] * inv_freq
    ang = jnp.concatenate([ang_r, ang_r, ang_c, ang_c], axis=1)
    ang = jnp.concatenate([ang, jnp.zeros((ctx, MLA_ROPE), F32)], axis=0)
    t_all = seq + ctx
    pad = jnp.zeros((t_all, HEAD_PAD - MLA_NOPE - MLA_ROPE), F32)
    cos_t = jnp.concatenate([jnp.ones((t_all, MLA_NOPE), F32), jnp.cos(ang), pad], axis=1)
    sin_t = jnp.concatenate([jnp.zeros((t_all, MLA_NOPE), F32), jnp.sin(ang), pad], axis=1)
    return cos_t, sin_t


def _dft_mats(t):
    r = np.arange(t, dtype=np.int64)
    ang = 2.0 * np.pi * ((r[:, None] * r[None, :]) % t) / t
    s = 1.0 / math.sqrt(t)
    return jnp.asarray(np.cos(ang) * s, BF16), jnp.asarray(-np.sin(ang) * s, BF16)


def _routing(top_idx, n_tok):
    n_assign = n_tok * TOP_K
    flat_e = top_idx.reshape(n_assign)
    rt = 1024
    assert n_assign % rt == 0
    onehot = (flat_e[:, None] == jnp.arange(N_EXPERTS, dtype=I32)[None, :])
    oh3 = onehot.reshape(n_assign // rt, rt, N_EXPERTS).astype(BF16)
    tri = jnp.asarray(np.tril(np.ones((rt, rt), np.float32)), BF16)
    within = jnp.einsum("ts,nse->nte", tri, oh3, preferred_element_type=F32)
    tile_cnt = within[:, -1, :]
    tile_off = jnp.cumsum(tile_cnt, axis=0) - tile_cnt
    csum = (within + tile_off[:, None, :]).reshape(n_assign, N_EXPERTS)
    counts = (tile_off[-1] + tile_cnt[-1]).astype(I32)
    rank = jnp.sum(jnp.where(onehot, csum, 0.0), axis=1).astype(I32) - 1
    padded = ((counts + MOE_BLOCK - 1) // MOE_BLOCK) * MOE_BLOCK
    pad_end = jnp.cumsum(padded)
    pad_start = pad_end - padded
    dest = (pad_start[flat_e] + rank).astype(I32)
    n_blocks = -(-n_assign // MOE_BLOCK) + N_EXPERTS
    n_rows = n_blocks * MOE_BLOCK
    row_tok = jnp.zeros((n_rows,), I32).at[dest].set(jnp.arange(n_assign, dtype=I32) // TOP_K,
                                                     unique_indices=True)
    blk_start = jnp.arange(n_blocks, dtype=I32) * MOE_BLOCK
    blk_e = jnp.minimum(jnp.sum((pad_end[None, :] <= blk_start[:, None]).astype(I32), axis=1), N_EXPERTS - 1)
    return blk_e, row_tok.reshape(n_blocks, MOE_BLOCK), dest


def kernel(x, c, ctx, c_ctx, w_mod, b_mod, norm1_g, w_in, mla_q_norm_g, mla_w_uq, mla_kv_norm_g, mla_w_ukv,
           gla_w_gate_f, gla_b_gate_f, gla_w_gate_b, gla_b_gate_b, gla_norm_g, w_br_mla, w_br_fnet, w_br_gla,
           w_o, norm2_g, router_w, router_b, exp_w_up, exp_b_up, exp_w_down, exp_b_down, final_norm_g):
    b, seq, d = x.shape
    n_ctx = ctx.shape[1]
    n_layers = w_mod.shape[0]
    t_all = seq + n_ctx
    assert d == D_MODEL and seq % n_ctx == 0 and seq % GRID_W == 0 and n_ctx % GLA_CHUNK == 0
    tm = min(256, n_ctx)
    assert n_ctx % tm == 0 and seq % tm == 0 and (tm * TOP_K) % MOE_BLOCK == 0
    n_lat_tiles = seq // tm
    t_fourier = min(512, seq)
    t_attn = min(512, seq)

    xa = jnp.concatenate([x, ctx], axis=1)

    rows = -(-(b + 1) // 8) * 8
    cv = jnp.zeros((rows, d), F32).at[0:b].set(c).at[b].set(c_ctx)
    mod_all = _mod_all(cv, w_mod, b_mod)
    cos_t, sin_t = _rope_tables(seq, n_ctx)
    cm_l, sm_l = _dft_mats(seq)
    cm_c, sm_c = _dft_mats(n_ctx)

    n_tok = b * t_all
    b_up4 = exp_b_up.reshape(n_layers, N_EXPERTS, 1, 2 * D_EXPERT)
    b_down4 = exp_b_down.reshape(n_layers, N_EXPERTS, 1, d)
    for l in range(n_layers):
        lw = _layer_weights(l, w_in, mla_q_norm_g, mla_w_uq, mla_kv_norm_g, mla_w_ukv, gla_w_gate_f,
                            gla_b_gate_f, gla_w_gate_b, gla_b_gate_b, gla_norm_g, w_br_mla, w_br_fnet,
                            w_br_gla, w_o, norm1_g, norm2_g, router_w, router_b)
        mod_lat = mod_all[l, 0:b]
        mod_ctx = jnp.broadcast_to(mod_all[l, b][None, :], (b, 6 * d))
        mod = jnp.stack([mod_lat, mod_ctx], axis=1).reshape(b, 2, 1, 6 * d)

        qc, kc, v, pq, gq, gk, gv, lfb, og, gate = _inproj(xa, mod, cos_t, sin_t, lw, tm, n_lat_tiles)
        o_mla_lat = _attention(qc, kc, v, t_attn, 0, seq, 0, t_all)
        o_mla_ctx = _attention(qc, kc, v, tm, seq, n_ctx, seq, n_ctx)
        y_fn_lat = _fourier(pq, cm_l, sm_l, t_fourier, 0)
        y_fn_ctx = _fourier(pq, cm_c, sm_c, n_ctx, seq)
        o_f, o_b = _gla(gq, gk, gv, lfb, tm, seq, n_ctx)
        xa, h2, top_idx, gates = _merge(xa, mod, o_mla_lat, o_mla_ctx, y_fn_lat, y_fn_ctx, o_f, o_b, og, gate,
                                        lw, tm, n_lat_tiles)

        blk_e, row_tok, dest = _routing(top_idx[..., 0:TOP_K], n_tok)
        y_sorted = _moe_experts(blk_e, row_tok, h2.reshape(n_tok, ROW_TILES, 128), exp_w_up, b_up4, exp_w_down,
                                b_down4, l)
        pos = dest.reshape(n_tok // tm, tm, TOP_K).transpose(0, 2, 1).reshape(n_tok // tm, tm * TOP_K)
        xa = _combine(xa, mod, gates, pos, y_sorted, tm, n_lat_tiles)

    return _final_norm(xa, final_norm_g, seq, tm)
```
